```python
import math
import jax
import jax.numpy as jnp
from jax import lax
import numpy as np

D_MODEL = 2048
BATCH = 32
SEQ = 256
DEPTH = 4
DEC_BATCH = 4
DEC_SEQ = 1024
PAST_LEN = 512

GRID_W = 64
N_EVEN = (DEPTH + 1) // 2
N_ODD = DEPTH // 2
EPS = 1e-6
H_A = 8
DK_A = 128
DV_A = 128
CONV_W = 5
H_B = 8
DK_B = 128
DV_B = 256
CHUNK = 64
H_C = 16
Q_RANK = 512
KV_RANK = 256
NOPE_DIM = 128
ROPE_DIM = 64
V_DIM = 128
Q_BLOCK = 128
ROPE_THETA = 10000.0
N_EXPERTS = 16
N_GROUPS = 4
EXPERTS_PER_GROUP = N_EXPERTS // N_GROUPS
TOPK_GROUPS = 1
TOP_K = 2
D_EXPERT = 512

kernel_name = 'hybrid_deltanet_retention_mla_moe_diffusion_step'


def rmsnorm(x, g):
    xf = x.astype(jnp.float32)
    y = xf * lax.rsqrt(jnp.mean(xf * xf, axis=-1, keepdims=True) + EPS)
    return (y * g.astype(jnp.float32)).astype(x.dtype)


def l2norm(x):
    xf = x.astype(jnp.float32)
    return (xf * lax.rsqrt(jnp.sum(xf * xf, axis=-1, keepdims=True) + EPS)).astype(x.dtype)


def head_groupnorm(o, g):
    B, N, H, d = o.shape
    of = o.astype(jnp.float32)
    mu = jnp.mean(of, axis=-1, keepdims=True)
    var = jnp.mean(jnp.square(of - mu), axis=-1, keepdims=True)
    y = ((of - mu) * lax.rsqrt(var + EPS)).reshape(B, N, H * d)
    return (y * g.astype(jnp.float32)).astype(o.dtype)


def adaln(cond, w_mod, b_mod):
    m = jax.nn.silu(cond) @ w_mod + b_mod
    return jnp.split(m[:, None, :], 6, axis=-1)


def modulated_norm(x, g, shift, scale):
    return rmsnorm(x, g) * (1.0 + scale) + shift


def rev(t):
    return jnp.flip(t, axis=1)


def axial_rope_tables(n_rows, rot_dim, dtype):
    row = jnp.repeat(jnp.arange(n_rows, dtype=jnp.float32), GRID_W)
    col = jnp.tile(jnp.arange(GRID_W, dtype=jnp.float32), n_rows)
    n_freq = rot_dim // 4
    inv = jnp.power(ROPE_THETA, -jnp.arange(n_freq, dtype=jnp.float32) / n_freq)
    ang = jnp.concatenate([row[:, None] * inv, col[:, None] * inv], axis=-1)
    return jnp.cos(ang).astype(dtype), jnp.sin(ang).astype(dtype)


def apply_rope(x, cos, sin):
    x1, x2 = jnp.split(x, 2, axis=-1)
    return jnp.concatenate([x1 * cos - x2 * sin, x2 * cos + x1 * sin], axis=-1)


def centred_dwconv(x, w):
    K, C = w.shape
    return lax.conv_general_dilated(x, w[:, None, :].astype(x.dtype), window_strides=(1,),
                                    padding=[(K // 2, K // 2)], dimension_numbers=('NWC', 'WIO', 'NWC'),
                                    feature_group_count=C)


def to_chunks(x):
    B, N, H, d = x.shape
    return x.reshape(B, N // CHUNK, CHUNK, H, d).transpose(1, 0, 3, 2, 4)


def from_chunks(x):
    n, B, H, C, d = x.shape
    return x.transpose(1, 0, 3, 2, 4).reshape(B, n * C, H, d)


def gate_chunks(g):
    B, N, H = g.shape
    return jnp.cumsum(g.reshape(B, N // CHUNK, CHUNK, H).transpose(1, 0, 3, 2), axis=-1)


def decay_matrix(gc):
    tri = jnp.tril(jnp.ones((CHUNK, CHUNK), dtype=bool))
    diff = gc[..., :, None] - gc[..., None, :]
    return jnp.where(tri, jnp.exp(jnp.where(tri, diff, 0.0)), 0.0)


def chunk_gated_delta(q, k, v, g, beta, s0):
    f32 = jnp.float32
    dtype = v.dtype
    qc = to_chunks(q.astype(f32)) * (q.shape[-1] ** -0.5)
    kc = to_chunks(k.astype(f32))
    vc = to_chunks(v.astype(f32))
    bc = to_chunks(beta.astype(f32)[..., None])
    gc = gate_chunks(g.astype(f32))
    decay = decay_matrix(gc)
    strict = jnp.tril(jnp.ones((CHUNK, CHUNK), f32), -1)
    lmat = jnp.einsum('nbhid,nbhjd->nbhij', kc * bc, kc) * decay * strict
    u = lax.linalg.triangular_solve(lmat, vc * bc, left_side=True, lower=True, unit_diagonal=True)
    w = lax.linalg.triangular_solve(lmat, kc * bc * jnp.exp(gc)[..., None], left_side=True, lower=True,
                                    unit_diagonal=True)
    a_intra = jnp.einsum('nbhid,nbhjd->nbhij', qc, kc) * decay

    def step(s, xs):
        q_i, k_i, u_i, w_i, a_i, gc_i = xs
        v_new = u_i - w_i @ s
        o = (q_i * jnp.exp(gc_i)[..., None]) @ s + a_i @ v_new
        g_last = gc_i[..., -1:]
        s = s * jnp.exp(g_last)[..., None] + jnp.einsum('bhcd,bhce->bhde', k_i * jnp.exp(g_last - gc_i)[..., None], v_new)
        return s, o

    s_fin, o = lax.scan(step, s0.astype(f32), (qc, kc, u, w, a_intra, gc))
    return from_chunks(o).astype(dtype), s_fin.astype(dtype)


def chunk_retention(q, k, v, log_gamma, s0):
    f32 = jnp.float32
    dtype = v.dtype
    B, N, H, DK = q.shape
    qc = to_chunks(q.astype(f32))
    kc = to_chunks(k.astype(f32)) * (DK ** -0.5)
    vc = to_chunks(v.astype(f32))
    gc = gate_chunks(jnp.broadcast_to(log_gamma.astype(f32), (B, N, H)))
    a_intra = jnp.einsum('nbhid,nbhjd->nbhij', qc, kc) * decay_matrix(gc)

    def step(s, xs):
        q_i, k_i, v_i, a_i, gc_i = xs
        o = (q_i * jnp.exp(gc_i)[..., None]) @ s + a_i @ v_i
        g_last = gc_i[..., -1:]
        s = s * jnp.exp(g_last)[..., None] + jnp.einsum('bhcd,bhce->bhde', k_i * jnp.exp(g_last - gc_i)[..., None], v_i)
        return s, o

    s_fin, o = lax.scan(step, s0.astype(f32), (qc, kc, vc, a_intra, gc))
    return from_chunks(o).astype(dtype), s_fin.astype(dtype)


def recurrent_pair_mixer(h, w_in, conv_w, a_log, dt_bias, dn_norm, ret_log_decay, ret_norm, w_out,
                         s_dn, s_ret, rope_tab):
    B, N, _ = h.shape
    sizes = [2 * H_A * DK_A + H_A * DV_A, 4 * H_A, H_A * DV_A, H_B * DK_B, H_B * DK_B, H_B * DV_B, H_B * DV_B]
    qkv_a, ab_a, gate_a, q_b, k_b, v_b, gate_b = jnp.split(h @ w_in, np.cumsum(sizes)[:-1].tolist(), axis=-1)
    qkv_a = jax.nn.silu(centred_dwconv(qkv_a, conv_w))
    q_a, k_a, v_a = jnp.split(qkv_a, [H_A * DK_A, 2 * H_A * DK_A], axis=-1)
    q_a = l2norm(q_a.reshape(B, N, H_A, DK_A))
    k_a = l2norm(k_a.reshape(B, N, H_A, DK_A))
    v_a = v_a.reshape(B, N, H_A, DV_A)
    ab = ab_a.reshape(B, N, 2, 2, H_A).astype(jnp.float32)
    g = -jnp.exp(a_log.astype(jnp.float32)) * jax.nn.softplus(ab[:, :, 0] + dt_bias.astype(jnp.float32))
    beta = jax.nn.sigmoid(ab[:, :, 1])
    o_f, sa_f = chunk_gated_delta(q_a, k_a, v_a, g[:, :, 0], beta[:, :, 0], s_dn[:, 0])
    o_b, sa_b = chunk_gated_delta(rev(q_a), rev(k_a), rev(v_a), rev(g[:, :, 1]), rev(beta[:, :, 1]), s_dn[:, 1])
    o_a = rmsnorm(o_f + rev(o_b), dn_norm) * jax.nn.silu(gate_a.reshape(B, N, H_A, DV_A))
    q_b = q_b.reshape(B, N, H_B, DK_B)
    k_b = k_b.reshape(B, N, H_B, DK_B)
    v_b = v_b.reshape(B, N, H_B, DV_B)
    if rope_tab is not None:
        cos, sin = rope_tab
        q_b = apply_rope(q_b, cos[:, None, :], sin[:, None, :])
        k_b = apply_rope(k_b, cos[:, None, :], sin[:, None, :])
    r_f, sr_f = chunk_retention(q_b, k_b, v_b, ret_log_decay[0], s_ret[:, 0])
    r_b, sr_b = chunk_retention(rev(q_b), rev(k_b), rev(v_b), ret_log_decay[1], s_ret[:, 1])
    o_r = head_groupnorm(r_f + rev(r_b), ret_norm) * jax.nn.silu(gate_b)
    y = jnp.concatenate([o_a.reshape(B, N, H_A * DV_A), o_r], axis=-1) @ w_out
    return y, jnp.stack([sa_f, sa_b], axis=1), jnp.stack([sr_f, sr_b], axis=1)


def blocked_attention(q_nope, q_rope, k_nope, k_rope, v):
    B, N, H, _ = q_nope.shape
    nb = N // Q_BLOCK
    scale = (NOPE_DIM + ROPE_DIM) ** -0.5

    def blk(t):
        return t.reshape(B, nb, Q_BLOCK, H, t.shape[-1]).swapaxes(0, 1)

    def attend(qs):
        qn, qr = qs
        s = jnp.einsum('bqhd,bkhd->bhqk', qn, k_nope) + jnp.einsum('bqhr,bkr->bhqk', qr, k_rope)
        p = jax.nn.softmax(s.astype(jnp.float32) * scale, axis=-1).astype(v.dtype)
        return jnp.einsum('bhqk,bkhd->bqhd', p, v)

    o = lax.map(attend, (blk(q_nope), blk(q_rope)))
    return o.swapaxes(0, 1).reshape(B, N, H, V_DIM)


def mla_mixer(h, w_in, q_norm, kv_norm, w_q_up, w_kv_up, w_out, ctx_ckv, ctx_krope, rope_tab):
    B, N, _ = h.shape
    cq, ckv, krope = jnp.split(h @ w_in, [Q_RANK, Q_RANK + KV_RANK], axis=-1)
    cq = rmsnorm(cq, q_norm)
    ckv = rmsnorm(ckv, kv_norm)
    q = (cq @ w_q_up).reshape(B, N, H_C, NOPE_DIM + ROPE_DIM)
    q_nope, q_rope = jnp.split(q, [NOPE_DIM], axis=-1)
    k_rope = krope
    if rope_tab is not None:
        cos, sin = rope_tab
        q_rope = apply_rope(q_rope, cos[:, None, :], sin[:, None, :])
        k_rope = apply_rope(krope, cos, sin)
    if ctx_ckv is not None:
        ckv_all = jnp.concatenate([ctx_ckv, ckv], axis=1)
        k_rope = jnp.concatenate([ctx_krope, k_rope], axis=1)
    else:
        ckv_all = ckv
    kv = (ckv_all @ w_kv_up).reshape(B, ckv_all.shape[1], H_C, NOPE_DIM + V_DIM)
    k_nope, v = jnp.split(kv, [NOPE_DIM], axis=-1)
    o = blocked_attention(q_nope, q_rope, k_nope, k_rope, v)
    return o.reshape(B, N, H_C * V_DIM) @ w_out, ckv, krope


def grouped_moe(h, router_w, router_b, w1, w3, w2):
    B, N, D = h.shape
    x = h.reshape(B * N, D)
    scores = jax.nn.sigmoid((x @ router_w).astype(jnp.float32))
    sel = scores + router_b.astype(jnp.float32)
    grp_top, _ = lax.top_k(sel.reshape(-1, N_GROUPS, EXPERTS_PER_GROUP), 2)
    _, gidx = lax.top_k(jnp.sum(grp_top, axis=-1), TOPK_GROUPS)
    gmask = jnp.any(gidx[:, :, None] == jnp.arange(N_GROUPS)[None, None, :], axis=1)
    emask = jnp.repeat(gmask, EXPERTS_PER_GROUP, axis=1)
    _, eidx = lax.top_k(jnp.where(emask, sel, -jnp.inf), TOP_K)
    wts = jnp.take_along_axis(scores, eidx, axis=1)
    wts = wts / jnp.sum(wts, axis=-1, keepdims=True)
    gates = jnp.sum(jax.nn.one_hot(eidx, N_EXPERTS, dtype=jnp.float32) * wts[..., None], axis=1).astype(x.dtype)
    y = jnp.zeros_like(x)
    for e in range(N_EXPERTS):
        he = jax.nn.silu(x @ w1[e]) * (x @ w3[e])
        y = y + gates[:, e:e + 1] * (he @ w2[e])
    return y.reshape(B, N, D)


def setup_inputs(seed: int = 0) -> dict:
    key = jax.random.key(seed)
    keys = iter(jax.random.split(key, 48))

    def nrm(shape, scale):
        return jax.random.normal(next(keys), shape, jnp.float32) * scale

    def gain(shape):
        return 1.0 + nrm(shape, 0.05)

    D = D_MODEL
    ev_in = 2 * H_A * DK_A + H_A * DV_A + 4 * H_A + H_A * DV_A + 2 * H_B * DK_B + 2 * H_B * DV_B
    ev_out = H_A * DV_A + H_B * DV_B
    conv_ch = 2 * H_A * DK_A + H_A * DV_A
    od_in = Q_RANK + KV_RANK + ROPE_DIM
    a_init = jax.random.uniform(next(keys), (N_EVEN, 2, H_A), dtype=jnp.float32, minval=1.0, maxval=16.0)
    dt = jnp.exp(jax.random.uniform(next(keys), (N_EVEN, 2, H_A), dtype=jnp.float32,
                                    minval=math.log(1e-3), maxval=math.log(1e-1)))
    ret_base = jnp.log1p(-jnp.power(2.0, -5.0 - jnp.arange(H_B, dtype=jnp.float32)))
    return {
        'x_prompt': nrm((BATCH, SEQ, D), 1.0),
        'x_sample': nrm((DEC_BATCH, DEC_SEQ, D), 1.0),
        'c': nrm((DEC_BATCH, D), 1.0),
        'state_deltanet': nrm((DEC_BATCH, N_EVEN, 2, H_A, DK_A, DV_A), 0.5),
        'state_retention': nrm((DEC_BATCH, N_EVEN, 2, H_B, DK_B, DV_B), 1.0),
        'cache_ckv': nrm((DEC_BATCH, N_ODD, PAST_LEN, KV_RANK), 1.0),
        'cache_krope': nrm((DEC_BATCH, N_ODD, PAST_LEN, ROPE_DIM), 1.0),
        'c_ctx': nrm((D,), 1.0),
        'mod_w': nrm((DEPTH, D, 6 * D), 0.5 * D ** -0.5),
        'mod_b': nrm((DEPTH, 6 * D), 0.02),
        'norm_mix': gain((DEPTH, D)),
        'norm_ffn': gain((DEPTH, D)),
        'norm_final': gain((D,)),
        'ev_w_in': nrm((N_EVEN, D, ev_in), D ** -0.5),
        'ev_conv': nrm((N_EVEN, CONV_W, conv_ch), CONV_W ** -0.5),
        'ev_a_log': jnp.log(a_init),
        'ev_dt_bias': dt + jnp.log(-jnp.expm1(-dt)),
        'ev_dn_norm': gain((N_EVEN, DV_A)),
        'ev_ret_log_decay': ret_base * (1.0 + nrm((N_EVEN, 2, H_B), 0.05)),
        'ev_ret_norm': gain((N_EVEN, H_B * DV_B)),
        'ev_w_out': nrm((N_EVEN, ev_out, D), ev_out ** -0.5),
        'od_w_in': nrm((N_ODD, D, od_in), D ** -0.5),
        'od_q_norm': gain((N_ODD, Q_RANK)),
        'od_kv_norm': gain((N_ODD, KV_RANK)),
        'od_w_q_up': nrm((N_ODD, Q_RANK, H_C * (NOPE_DIM + ROPE_DIM)), Q_RANK ** -0.5),
        'od_w_kv_up': nrm((N_ODD, KV_RANK, H_C * (NOPE_DIM + V_DIM)), KV_RANK ** -0.5),
        'od_w_out': nrm((N_ODD, H_C * V_DIM, D), (H_C * V_DIM) ** -0.5),
        'router_w': nrm((D, N_EXPERTS), D ** -0.5),
        'router_b': nrm((N_EXPERTS,), 0.01),
        'moe_w1': nrm((DEPTH, N_EXPERTS, D, D_EXPERT), D ** -0.5),
        'moe_w3': nrm((DEPTH, N_EXPERTS, D, D_EXPERT), D ** -0.5),
        'moe_w2': nrm((DEPTH, N_EXPERTS, D_EXPERT, D), D_EXPERT ** -0.5),
    }


def reference(x_prompt, x_sample, c, state_deltanet, state_retention, cache_ckv, cache_krope, c_ctx,
              mod_w, mod_b, norm_mix, norm_ffn, norm_final,
              ev_w_in, ev_conv, ev_a_log, ev_dt_bias, ev_dn_norm, ev_ret_log_decay, ev_ret_norm, ev_w_out,
              od_w_in, od_q_norm, od_kv_norm, od_w_q_up, od_w_kv_up, od_w_out,
              router_w, router_b, moe_w1, moe_w3, moe_w2):
    n_rows = x_sample.shape[1] // GRID_W
    rope_ret = axial_rope_tables(n_rows, DK_B, x_sample.dtype)
    rope_mla = axial_rope_tables(n_rows, ROPE_DIM, x_sample.dtype)
    b_ctx = x_prompt.shape[0]
    zero_dn = jnp.zeros((b_ctx, 2, H_A, DK_A, DV_A), x_prompt.dtype)
    zero_ret = jnp.zeros((b_ctx, 2, H_B, DK_B, DV_B), x_prompt.dtype)
    xp = x_prompt
    xs = x_sample
    new_dn, new_ret, new_ckv, new_krope = [], [], [], []
    for l in range(DEPTH):
        sh1p, sc1p, gt1p, sh2p, sc2p, gt2p = adaln(c_ctx[None, :], mod_w[l], mod_b[l])
        sh1s, sc1s, gt1s, sh2s, sc2s, gt2s = adaln(c, mod_w[l], mod_b[l])
        hp = modulated_norm(xp, norm_mix[l], sh1p, sc1p)
        hs = modulated_norm(xs, norm_mix[l], sh1s, sc1s)
        if l % 2 == 0:
            e = l // 2
            wts = (ev_w_in[e], ev_conv[e], ev_a_log[e], ev_dt_bias[e], ev_dn_norm[e], ev_ret_log_decay[e],
                   ev_ret_norm[e], ev_w_out[e])
            yp, s_dn, s_ret = recurrent_pair_mixer(hp, *wts, zero_dn, zero_ret, None)
            ys, _, _ = recurrent_pair_mixer(hs, *wts, state_deltanet[:, e], state_retention[:, e], rope_ret)
            new_dn.append(s_dn)
            new_ret.append(s_ret)
        else:
            o = l // 2
            wts = (od_w_in[o], od_q_norm[o], od_kv_norm[o], od_w_q_up[o], od_w_kv_up[o], od_w_out[o])
            yp, ckv, krope = mla_mixer(hp, *wts, None, None, None)
            ys, _, _ = mla_mixer(hs, *wts, cache_ckv[:, o], cache_krope[:, o], rope_mla)
            new_ckv.append(ckv)
            new_krope.append(krope)
        xp = xp + gt1p * yp
        xs = xs + gt1s * ys
        hp = modulated_norm(xp, norm_ffn[l], sh2p, sc2p)
        hs = modulated_norm(xs, norm_ffn[l], sh2s, sc2s)
        xp = xp + gt2p * grouped_moe(hp, router_w, router_b, moe_w1[l], moe_w3[l], moe_w2[l])
        xs = xs + gt2s * grouped_moe(hs, router_w, router_b, moe_w1[l], moe_w3[l], moe_w2[l])
    y_prompt = rmsnorm(xp, norm_final)
    y_sample = rmsnorm(xs, norm_final)
    return (y_prompt, y_sample, jnp.stack(new_dn, axis=1), jnp.stack(new_ret, axis=1),
            jnp.stack(new_ckv, axis=1), jnp.stack(new_krope, axis=1))
```

```python
import functools
import math

import numpy as np
import jax
import jax.numpy as jnp
from jax import lax
from jax.experimental import pallas as pl
from jax.experimental.pallas import tpu as pltpu

F32 = jnp.float32
BF16 = jnp.bfloat16

D_MODEL = 2048
BATCH = 32
SEQ = 256
DEPTH = 4
DEC_BATCH = 4
DEC_SEQ = 1024
PAST_LEN = 512
GRID_W = 64
EPS = 1e-6
H_A = 8
DK_A = 128
DV_A = 128
CONV_W = 5
H_B = 8
DK_B = 128
DV_B = 256
H_C = 16
Q_RANK = 512
KV_RANK = 256
NOPE_DIM = 128
ROPE_DIM = 64
V_DIM = 128
ROPE_THETA = 10000.0
N_EXPERTS = 16
N_GROUPS = 4
EXPERTS_PER_GROUP = N_EXPERTS // N_GROUPS
TOP_K = 2
D_EXPERT = 512

P_ROWS = BATCH * SEQ
S_ROWS = DEC_BATCH * DEC_SEQ
ROWS = P_ROWS + S_ROWS
N_COND = 8

ROW_TILE = 512
VMEM_LIMIT = 48 * 1024 * 1024
CHUNK = 64
MOE_TILE = 256


def _cparams(sem):
    return pltpu.CompilerParams(dimension_semantics=sem, vmem_limit_bytes=VMEM_LIMIT)


def _cond_of_tile(i, tm):
    p_tiles = P_ROWS // tm
    per_req = DEC_SEQ // tm
    return jnp.where(i < p_tiles, 0, 1 + (i - p_tiles) // per_req)


def _mods_kernel(c_ref, w_ref, b_ref, o_ref):
    c = c_ref[...]
    s = (c * jax.nn.sigmoid(c)).astype(BF16)
    o_ref[0] = jnp.dot(s, w_ref[0].astype(BF16), preferred_element_type=F32) + b_ref[0]


def _mods(cond8, mod_w, mod_b):
    tn = 1024
    n = mod_w.shape[-1]
    return pl.pallas_call(
        _mods_kernel,
        grid=(DEPTH, n // tn),
        in_specs=[
            pl.BlockSpec((N_COND, D_MODEL), lambda l, j: (0, 0)),
            pl.BlockSpec((1, D_MODEL, tn), lambda l, j: (l, 0, j)),
            pl.BlockSpec((1, 1, tn), lambda l, j: (l, 0, j)),
        ],
        out_specs=pl.BlockSpec((1, N_COND, tn), lambda l, j: (l, 0, j)),
        out_shape=jax.ShapeDtypeStruct((DEPTH, N_COND, n), F32),
        compiler_params=_cparams(("arbitrary", "arbitrary")),
        name="adaln_mods",
    )(cond8, mod_w, mod_b.reshape(DEPTH, 1, n))


def _modnorm_kernel(x_ref, g_ref, sh_ref, sc_ref, o_ref):
    x = x_ref[...]
    y = x * lax.rsqrt(jnp.mean(x * x, axis=-1, keepdims=True) + EPS)
    y = y * g_ref[0]
    o_ref[...] = (y * (1.0 + sc_ref[0, 0]) + sh_ref[0, 0]).astype(o_ref.dtype)


def _modnorm(x, gain, mods4, l, seg_shift, seg_scale):
    tm = ROW_TILE
    return pl.pallas_call(
        _modnorm_kernel,
        grid=(ROWS // tm,),
        in_specs=[
            pl.BlockSpec((tm, D_MODEL), lambda i: (i, 0)),
            pl.BlockSpec((1, 1, D_MODEL), lambda i: (l, 0, 0)),
            pl.BlockSpec((1, 1, 1, D_MODEL), lambda i: (l, _cond_of_tile(i, tm), 0, seg_shift)),
            pl.BlockSpec((1, 1, 1, D_MODEL), lambda i: (l, _cond_of_tile(i, tm), 0, seg_scale)),
        ],
        out_specs=pl.BlockSpec((tm, D_MODEL), lambda i: (i, 0)),
        out_shape=jax.ShapeDtypeStruct((ROWS, D_MODEL), BF16),
        compiler_params=_cparams(("arbitrary",)),
        name="modnorm",
    )(x, gain, mods4, mods4)


def _rmsnorm_kernel(x_ref, g_ref, o_ref):
    x = x_ref[...]
    y = x * lax.rsqrt(jnp.mean(x * x, axis=-1, keepdims=True) + EPS)
    o_ref[...] = y * g_ref[...]


def _final_norm(x, gain):
    tm = ROW_TILE
    return pl.pallas_call(
        _rmsnorm_kernel,
        grid=(ROWS // tm,),
        in_specs=[pl.BlockSpec((tm, D_MODEL), lambda i: (i, 0)),
                  pl.BlockSpec((1, D_MODEL), lambda i: (0, 0))],
        out_specs=pl.BlockSpec((tm, D_MODEL), lambda i: (i, 0)),
        out_shape=jax.ShapeDtypeStruct((ROWS, D_MODEL), F32),
        compiler_params=_cparams(("arbitrary",)),
        name="final_norm",
    )(x, gain.reshape(1, D_MODEL))


def _load_b(b_ref):
    return b_ref[0] if len(b_ref.shape) == 3 else b_ref[...]


def _mm_kernel(a_ref, b_ref, o_ref, bscr):
    @pl.when(pl.program_id(1) == 0)
    def _():
        bscr[...] = _load_b(b_ref).astype(BF16)

    o_ref[...] = jnp.dot(a_ref[...], bscr[...], preferred_element_type=F32).astype(o_ref.dtype)


def _mm_res_kernel(a_ref, b_ref, x_ref, gt_ref, o_ref, bscr):
    @pl.when(pl.program_id(1) == 0)
    def _():
        bscr[...] = _load_b(b_ref).astype(BF16)

    y = jnp.dot(a_ref[...], bscr[...], preferred_element_type=F32)
    o_ref[...] = x_ref[...] + gt_ref[0, 0] * y


def _matmul(a, b, *, lead=None, out_dtype=F32, tm=ROW_TILE, tn=1024, name="matmul"):
    m, k = a.shape
    n = b.shape[-1]
    tn = min(tn, n)
    assert m % tm == 0 and n % tn == 0
    if lead is None:
        b_spec = pl.BlockSpec((k, tn), lambda j, i: (0, j))
    else:
        b_spec = pl.BlockSpec((1, k, tn), lambda j, i: (lead, 0, j))
    return pl.pallas_call(
        _mm_kernel,
        grid=(n // tn, m // tm),
        in_specs=[pl.BlockSpec((tm, k), lambda j, i: (i, 0)), b_spec],
        out_specs=pl.BlockSpec((tm, tn), lambda j, i: (i, j)),
        out_shape=jax.ShapeDtypeStruct((m, n), out_dtype),
        scratch_shapes=[pltpu.VMEM((k, tn), BF16)],
        compiler_params=_cparams(("arbitrary", "arbitrary")),
        name=name,
    )(a, b)


def _matmul_residual(a, b, lead, x, mods4, l, seg_gate, name):
    m, k = a.shape
    n = b.shape[-1]
    tm, tn = ROW_TILE, 1024
    return pl.pallas_call(
        _mm_res_kernel,
        grid=(n // tn, m // tm),
        in_specs=[
            pl.BlockSpec((tm, k), lambda j, i: (i, 0)),
            pl.BlockSpec((1, k, tn), lambda j, i: (lead, 0, j)),
            pl.BlockSpec((tm, tn), lambda j, i: (i, j)),
            pl.BlockSpec((1, 1, 1, tn),
                         lambda j, i: (l, _cond_of_tile(i, tm), 0, seg_gate * (D_MODEL // tn) + j)),
        ],
        out_specs=pl.BlockSpec((tm, tn), lambda j, i: (i, j)),
        out_shape=jax.ShapeDtypeStruct((m, n), F32),
        scratch_shapes=[pltpu.VMEM((k, tn), BF16)],
        compiler_params=_cparams(("arbitrary", "arbitrary")),
        name=name,
    )(a, b, x, mods4)


def _bf(x):
    return x.astype(BF16)


def _dot(a, b):
    return jnp.dot(_bf(a), _bf(b), preferred_element_type=F32)


def _dot_nt(a, b):
    return lax.dot_general(_bf(a), _bf(b), (((1,), (1,)), ((), ())), preferred_element_type=F32)


def _dot_tn(a, b):
    return lax.dot_general(_bf(a), _bf(b), (((0,), (0,)), ((), ())), preferred_element_type=F32)


def _dot_sel(m01, x):
    hi = _bf(x)
    r1 = x - hi.astype(F32)
    mid = _bf(r1)
    lo = _bf(r1 - mid.astype(F32))
    mb = _bf(m01)
    acc = jnp.dot(mb, lo, preferred_element_type=F32)
    acc = acc + jnp.dot(mb, mid, preferred_element_type=F32)
    return acc + jnp.dot(mb, hi, preferred_element_type=F32)


def _split2(x):
    hi = _bf(x)
    return hi, _bf(x - hi.astype(F32))


def _dot_x3(a, b):
    ah, al = _split2(a)
    bh, bl = _split2(b)
    acc = jnp.dot(al, bh, preferred_element_type=F32)
    acc = acc + jnp.dot(ah, bl, preferred_element_type=F32)
    return acc + jnp.dot(ah, bh, preferred_element_type=F32)


def _chunk_masks(c, rev):
    row = lax.broadcasted_iota(jnp.int32, (c, c), 0)
    col = lax.broadcasted_iota(jnp.int32, (c, c), 1)
    incl = (row <= col) if rev else (row >= col)
    strict = (row < col) if rev else (row > col)
    return row, col, incl, strict


def _delta_kernel(q_ref, k_ref, v_ref, g_ref, b_ref, s0_ref, o_ref, sfin_ref,
                  u_scr, w_scr, a_scr, qd_scr, kd_scr, *, n_chunks, rev):
    c = CHUNK
    row, col, incl, strict = _chunk_masks(c, rev)
    incl_f = incl.astype(F32)
    incl_t_f = ((row >= col) if rev else (row <= col)).astype(F32)
    ones_f = jnp.ones((c, c), F32)
    eye_f = (row == col).astype(F32)
    scale = DK_A ** -0.5

    def prep(ci, carry):
        sl = pl.ds(pl.multiple_of(ci * c, c), c)
        q = q_ref[0, sl, :]
        k = k_ref[0, sl, :]
        v = v_ref[0, sl, :]
        g = g_ref[0, 0, sl, :]
        beta = b_ref[0, 0, sl, :]
        g_bc = jnp.broadcast_to(g, (c, c))
        gc_col = _dot_sel(incl_f, g_bc)
        gc_row = _dot_sel(ones_f, g_bc * incl_t_f)
        decay = jnp.where(incl, jnp.exp(jnp.where(incl, gc_col - gc_row, 0.0)), 0.0)
        gc1 = gc_col[:, :1]
        g_tot = jnp.sum(g, axis=0, keepdims=True)
        kb = k * beta
        lmat = _dot_nt(kb, k) * jnp.where(strict, decay, 0.0)
        p = -lmat
        t = eye_f + p
        for _ in range(int(math.log2(c)) - 1):
            p = _dot_x3(p, p)
            t = t + _dot_x3(t, p)
        u = _dot_x3(t, v * beta)
        w = _dot_x3(t, kb * jnp.exp(gc1))
        qs = q * scale
        a = _dot_nt(qs, k) * decay
        u_scr[sl, :] = u
        w_scr[sl, :] = _bf(w)
        a_scr[sl, :] = _bf(a)
        qd_scr[sl, :] = _bf(qs * jnp.exp(gc1))
        kd_scr[sl, :] = _bf(k * jnp.exp(g_tot - gc1))
        return carry

    lax.fori_loop(0, n_chunks, prep, 0)

    def scan(i, s):
        ci = (n_chunks - 1 - i) if rev else i
        sl = pl.ds(pl.multiple_of(ci * c, c), c)
        g_tot = jnp.sum(g_ref[0, 0, sl, :], axis=0, keepdims=True)
        sb = _bf(s)
        v_new = u_scr[sl, :] - jnp.dot(w_scr[sl, :], sb, preferred_element_type=F32)
        vb = _bf(v_new)
        o = (jnp.dot(qd_scr[sl, :], sb, preferred_element_type=F32)
             + jnp.dot(a_scr[sl, :], vb, preferred_element_type=F32))
        o_ref[0, sl, :] = o
        return s * jnp.exp(g_tot) + lax.dot_general(
            kd_scr[sl, :], vb, (((0,), (0,)), ((), ())), preferred_element_type=F32)

    s_fin = lax.fori_loop(0, n_chunks, scan, s0_ref[0, 0])
    sfin_ref[0, 0] = s_fin


def _delta_rule(q, k, v, g, beta, s0, rev):
    b, n, _ = q.shape
    n_chunks = n // CHUNK
    hd = lambda bi, hi: (bi, 0, hi)
    gb = lambda bi, hi: (bi, hi, 0, 0)
    return pl.pallas_call(
        functools.partial(_delta_kernel, n_chunks=n_chunks, rev=rev),
        grid=(b, H_A),
        in_specs=[
            pl.BlockSpec((1, n, DK_A), hd),
            pl.BlockSpec((1, n, DK_A), hd),
            pl.BlockSpec((1, n, DV_A), hd),
            pl.BlockSpec((1, 1, n, 1), gb),
            pl.BlockSpec((1, 1, n, 1), gb),
            pl.BlockSpec((1, 1, DK_A, DV_A), gb),
        ],
        out_specs=[pl.BlockSpec((1, n, DV_A), hd), pl.BlockSpec((1, 1, DK_A, DV_A), gb)],
        out_shape=[jax.ShapeDtypeStruct((b, n, H_A * DV_A), F32),
                   jax.ShapeDtypeStruct((b, H_A, DK_A, DV_A), F32)],
        scratch_shapes=[
            pltpu.VMEM((n, DV_A), F32),
            pltpu.VMEM((n, DK_A), BF16),
            pltpu.VMEM((n, CHUNK), BF16),
            pltpu.VMEM((n, DK_A), BF16),
            pltpu.VMEM((n, DK_A), BF16),
        ],
        compiler_params=_cparams(("arbitrary", "arbitrary")),
        name="delta_rev" if rev else "delta_fwd",
    )(q, k, v, g, beta, s0)


def _ret_kernel(q_ref, k_ref, v_ref, lg_ref, s0_ref, o_ref, sfin_ref, *, n_chunks, rev):
    c = CHUNK
    row, col, incl, _ = _chunk_masks(c, rev)
    lg = lg_ref[0]
    dist = jnp.abs(row - col).astype(F32)
    decay = jnp.where(incl, jnp.exp(lg * dist), 0.0)
    pos = lax.broadcasted_iota(jnp.int32, (c, 1), 0)
    steps = ((c - pos) if rev else (pos + 1)).astype(F32)
    e_in = jnp.exp(lg * steps)
    e_out = jnp.exp(lg * (c - steps))
    e_tot = jnp.exp(lg * c)
    kscale = DK_B ** -0.5

    def scan(i, s):
        ci = (n_chunks - 1 - i) if rev else i
        sl = pl.ds(pl.multiple_of(ci * c, c), c)
        q = q_ref[0, sl, :]
        k = k_ref[0, sl, :] * kscale
        vb = _bf(v_ref[0, sl, :])
        a = _dot_nt(q, k) * decay
        o = _dot(q * e_in, s) + jnp.dot(_bf(a), vb, preferred_element_type=F32)
        o_ref[0, sl, :] = o
        return s * e_tot + lax.dot_general(_bf(k * e_out), vb, (((0,), (0,)), ((), ())),
                                           preferred_element_type=F32)

    s_fin = lax.fori_loop(0, n_chunks, scan, s0_ref[0, 0])
    sfin_ref[0, 0] = s_fin


def _retention(q, k, v, lg, s0, rev):
    b, n, _ = q.shape
    n_chunks = n // CHUNK
    hd = lambda bi, hi: (bi, 0, hi)
    st = lambda bi, hi: (bi, hi, 0, 0)
    return pl.pallas_call(
        functools.partial(_ret_kernel, n_chunks=n_chunks, rev=rev),
        grid=(b, H_B),
        in_specs=[
            pl.BlockSpec((1, n, DK_B), hd),
            pl.BlockSpec((1, n, DK_B), hd),
            pl.BlockSpec((1, n, DV_B), hd),
            pl.BlockSpec((1, 1, 1), lambda bi, hi: (hi, 0, 0)),
            pl.BlockSpec((1, 1, DK_B, DV_B), st),
        ],
        out_specs=[pl.BlockSpec((1, n, DV_B), hd), pl.BlockSpec((1, 1, DK_B, DV_B), st)],
        out_shape=[jax.ShapeDtypeStruct((b, n, H_B * DV_B), F32),
                   jax.ShapeDtypeStruct((b, H_B, DK_B, DV_B), F32)],
        compiler_params=_cparams(("arbitrary", "arbitrary")),
        name="ret_rev" if rev else "ret_fwd",
    )(q, k, v, lg, s0)


def _attn_kernel(qn_ref, qr_ref, kv_ref, kr_ref, o_ref):
    scale = (NOPE_DIM + ROPE_DIM) ** -0.5
    kr = kr_ref[0]
    for h in range(H_C):
        qn = qn_ref[0, :, h * NOPE_DIM:(h + 1) * NOPE_DIM]
        qr = qr_ref[0, :, h * ROPE_DIM:(h + 1) * ROPE_DIM]
        kn = kv_ref[0, :, h * 256:h * 256 + NOPE_DIM]
        vh = kv_ref[0, :, h * 256 + NOPE_DIM:(h + 1) * 256]
        s = (lax.dot_general(qn, kn, (((1,), (1,)), ((), ())), preferred_element_type=F32)
             + lax.dot_general(qr, kr, (((1,), (1,)), ((), ())), preferred_element_type=F32)) * scale
        m = jnp.max(s, axis=-1, keepdims=True)
        e = jnp.exp(s - m)
        p = e / jnp.sum(e, axis=-1, keepdims=True)
        o_ref[0, :, h * V_DIM:(h + 1) * V_DIM] = jnp.dot(
            _bf(p), vh, preferred_element_type=F32).astype(o_ref.dtype)


def _attention(qn, qr, kv, kr):
    b, nq, _ = qn.shape
    nk = kv.shape[1]
    tq = 256
    return pl.pallas_call(
        _attn_kernel,
        grid=(b, nq // tq),
        in_specs=[
            pl.BlockSpec((1, tq, H_C * NOPE_DIM), lambda bi, qi: (bi, qi, 0)),
            pl.BlockSpec((1, tq, H_C * ROPE_DIM), lambda bi, qi: (bi, qi, 0)),
            pl.BlockSpec((1, nk, H_C * 256), lambda bi, qi: (bi, 0, 0)),
            pl.BlockSpec((1, nk, ROPE_DIM), lambda bi, qi: (bi, 0, 0)),
        ],
        out_specs=pl.BlockSpec((1, tq, H_C * V_DIM), lambda bi, qi: (bi, qi, 0)),
        out_shape=jax.ShapeDtypeStruct((b, nq, H_C * V_DIM), BF16),
        compiler_params=_cparams(("arbitrary", "arbitrary")),
        name="mla_attention",
    )(qn, qr, kv, kr)


def _moe_kernel(te_ref, nv_ref, x_ref, w1_ref, w3_ref, w2_ref, gw_ref, o_ref, w1s, w3s, w2s):
    i = pl.program_id(0)
    prev = te_ref[jnp.maximum(i - 1, 0)]

    @pl.when((i == 0) | (te_ref[i] != prev))
    def _():
        w1s[...] = w1_ref[0, 0].astype(BF16)
        w3s[...] = w3_ref[0, 0].astype(BF16)
        w2s[...] = w2_ref[0, 0].astype(BF16)

    @pl.when(i < nv_ref[0])
    def _():
        x = x_ref[...]
        h1 = jnp.dot(x, w1s[...], preferred_element_type=F32)
        h3 = jnp.dot(x, w3s[...], preferred_element_type=F32)
        he = (h1 * jax.nn.sigmoid(h1)) * h3
        y = jnp.dot(_bf(he), w2s[...], preferred_element_type=F32)
        o_ref[...] = gw_ref[...] * y

    @pl.when(i >= nv_ref[0])
    def _():
        o_ref[...] = jnp.zeros_like(o_ref)


def _moe_experts(xg, gw, tile_expert, n_valid, w1, w3, w2, l):
    r = xg.shape[0]
    tm = MOE_TILE
    wmap = lambda i, te, nv: (l, te[i], 0, 0)
    grid_spec = pltpu.PrefetchScalarGridSpec(
        num_scalar_prefetch=2,
        grid=(r // tm,),
        in_specs=[
            pl.BlockSpec((tm, D_MODEL), lambda i, te, nv: (i, 0)),
            pl.BlockSpec((1, 1, D_MODEL, D_EXPERT), wmap),
            pl.BlockSpec((1, 1, D_MODEL, D_EXPERT), wmap),
            pl.BlockSpec((1, 1, D_EXPERT, D_MODEL), wmap),
            pl.BlockSpec((tm, 1), lambda i, te, nv: (i, 0)),
        ],
        out_specs=pl.BlockSpec((tm, D_MODEL), lambda i, te, nv: (i, 0)),
        scratch_shapes=[
            pltpu.VMEM((D_MODEL, D_EXPERT), BF16),
            pltpu.VMEM((D_MODEL, D_EXPERT), BF16),
            pltpu.VMEM((D_EXPERT, D_MODEL), BF16),
        ],
    )
    return pl.pallas_call(
        _moe_kernel,
        grid_spec=grid_spec,
        out_shape=jax.ShapeDtypeStruct((r, D_MODEL), F32),
        compiler_params=_cparams(("arbitrary",)),
        name="moe_experts",
    )(tile_expert, n_valid, xg, w1, w3, w2, gw)


def _route(logits, router_b):
    scores = jax.nn.sigmoid(logits)
    sel = scores + router_b.astype(F32)
    grp_top, _ = lax.top_k(sel.reshape(-1, N_GROUPS, EXPERTS_PER_GROUP), 2)
    _, gidx = lax.top_k(jnp.sum(grp_top, axis=-1), 1)
    gmask = gidx[:, :, None] == jnp.arange(N_GROUPS)[None, None, :]
    emask = jnp.repeat(jnp.any(gmask, axis=1), EXPERTS_PER_GROUP, axis=1)
    _, eidx = lax.top_k(jnp.where(emask, sel, -jnp.inf), TOP_K)
    wts = jnp.take_along_axis(scores, eidx, axis=1)
    wts = wts / jnp.sum(wts, axis=-1, keepdims=True)
    return eidx, wts


def _moe(h, router_w_pad, router_b, w1, w3, w2, l):
    t = h.shape[0]
    tm = MOE_TILE
    logits = _matmul(h, router_w_pad, tn=128, name="router")[:, :N_EXPERTS]
    eidx, wts = _route(logits, router_b)
    flat_e = eidx.reshape(-1)
    order = jnp.argsort(flat_e, stable=True)
    counts = jnp.bincount(flat_e, length=N_EXPERTS)
    padded = ((counts + tm - 1) // tm) * tm
    pad_start = jnp.cumsum(padded) - padded
    start = jnp.cumsum(counts) - counts
    sorted_e = flat_e[order]
    rank = jnp.arange(2 * t) - start[sorted_e]
    dest = pad_start[sorted_e] + rank
    n_rows = 2 * t + N_EXPERTS * tm
    src_tok = jnp.zeros((n_rows,), jnp.int32).at[dest].set((order // 2).astype(jnp.int32))
    gw = jnp.zeros((n_rows,), F32).at[dest].set(wts.reshape(-1)[order])
    pos = jnp.zeros((2 * t,), jnp.int32).at[order].set(dest.astype(jnp.int32)).reshape(t, 2)
    n_tiles = n_rows // tm
    tile_start = jnp.arange(n_tiles) * tm
    pad_end = jnp.cumsum(padded)
    te = jnp.sum(tile_start[:, None] >= pad_end[None, :], axis=1)
    n_valid = (pad_end[-1] // tm).astype(jnp.int32)
    last_e = jnp.max(jnp.where(counts > 0, jnp.arange(N_EXPERTS), 0))
    te = jnp.where(jnp.arange(n_tiles) < n_valid, te, last_e).astype(jnp.int32)
    xg = h[src_tok]
    out = _moe_experts(xg, gw[:, None], te, n_valid.reshape(1), w1, w3, w2, l)
    return out[pos[:, 0]] + out[pos[:, 1]]


def _l2norm(x):
    return x * lax.rsqrt(jnp.sum(x * x, axis=-1, keepdims=True) + EPS)


def _rope_tables(n_rows, rot_dim):
    row = jnp.repeat(jnp.arange(n_rows, dtype=F32), GRID_W)
    col = jnp.tile(jnp.arange(GRID_W, dtype=F32), n_rows)
    n_freq = rot_dim // 4
    inv = jnp.power(ROPE_THETA, -jnp.arange(n_freq, dtype=F32) / n_freq)
    ang = jnp.concatenate([row[:, None] * inv, col[:, None] * inv], axis=-1)
    return jnp.cos(ang), jnp.sin(ang)


def _apply_rope(x, cos, sin):
    x1, x2 = jnp.split(x, 2, axis=-1)
    return jnp.concatenate([x1 * cos - x2 * sin, x2 * cos + x1 * sin], axis=-1)


def _dwconv(x, w):
    k, c = w.shape
    return lax.conv_general_dilated(x, w[:, None, :], window_strides=(1,), padding=[(k // 2, k // 2)],
                                    dimension_numbers=('NWC', 'WIO', 'NWC'), feature_group_count=c)


def _pair_mixer(proj, ab, conv_w, a_log, dt_bias, dn_norm, ret_log_decay, ret_norm, s_dn, s_ret, rope_tab):
    b, n, _ = proj.shape
    qkv = jax.nn.silu(_dwconv(proj[..., :3072], conv_w))
    q_a = _l2norm(qkv[..., :1024].reshape(b, n, H_A, DK_A)).reshape(b, n, 1024)
    k_a = _l2norm(qkv[..., 1024:2048].reshape(b, n, H_A, DK_A)).reshape(b, n, 1024)
    v_a = qkv[..., 2048:3072]
    gate_a = proj[..., 3072:4096]
    abr = ab.reshape(b, n, 2, 2, H_A)
    g = -jnp.exp(a_log) * jax.nn.softplus(abr[:, :, 0] + dt_bias)
    beta = jax.nn.sigmoid(abr[:, :, 1])
    gcol = lambda t, d: jnp.transpose(t[:, :, d], (0, 2, 1))[..., None]
    o_f, sa_f = _delta_rule(q_a, k_a, v_a, gcol(g, 0), gcol(beta, 0), s_dn[:, 0], False)
    o_b, sa_b = _delta_rule(q_a, k_a, v_a, gcol(g, 1), gcol(beta, 1), s_dn[:, 1], True)
    o = (o_f + o_b).reshape(b, n, H_A, DV_A)
    o = o * lax.rsqrt(jnp.mean(o * o, axis=-1, keepdims=True) + EPS) * dn_norm
    o_a = o.reshape(b, n, 1024) * jax.nn.silu(gate_a)

    q_b = proj[..., 4096:5120]
    k_b = proj[..., 5120:6144]
    v_b = proj[..., 6144:8192]
    gate_b = proj[..., 8192:10240]
    if rope_tab is not None:
        cos, sin = rope_tab
        q_b = _apply_rope(q_b.reshape(b, n, H_B, DK_B), cos[:, None, :], sin[:, None, :]).reshape(b, n, 1024)
        k_b = _apply_rope(k_b.reshape(b, n, H_B, DK_B), cos[:, None, :], sin[:, None, :]).reshape(b, n, 1024)
    r_f, sr_f = _retention(q_b, k_b, v_b, ret_log_decay[0].reshape(H_B, 1, 1), s_ret[:, 0], False)
    r_b, sr_b = _retention(q_b, k_b, v_b, ret_log_decay[1].reshape(H_B, 1, 1), s_ret[:, 1], True)
    r = (r_f + r_b).reshape(b, n, H_B, DV_B)
    mu = jnp.mean(r, axis=-1, keepdims=True)
    var = jnp.mean(jnp.square(r - mu), axis=-1, keepdims=True)
    o_r = ((r - mu) * lax.rsqrt(var + EPS)).reshape(b, n, H_B * DV_B) * ret_norm * jax.nn.silu(gate_b)
    y_in = jnp.concatenate([o_a, o_r], axis=-1).astype(BF16)
    return y_in, jnp.stack([sa_f, sa_b], axis=1), jnp.stack([sr_f, sr_b], axis=1)


def _rms(x, g):
    return x * lax.rsqrt(jnp.mean(x * x, axis=-1, keepdims=True) + EPS) * g


def kernel(x_prompt, x_sample, c, state_deltanet, state_retention, cache_ckv, cache_krope, c_ctx, mod_w, mod_b, norm_mix, norm_ffn, norm_final, ev_w_in, ev_conv, ev_a_log, ev_dt_bias, ev_dn_norm, ev_ret_log_decay, ev_ret_norm, ev_w_out, od_w_in, od_q_norm, od_kv_norm, od_w_q_up, od_w_kv_up, od_w_out, router_w, router_b, moe_w1, moe_w3, moe_w2):
    x = jnp.concatenate([x_prompt.reshape(P_ROWS, D_MODEL), x_sample.reshape(S_ROWS, D_MODEL)], axis=0)
    cond8 = jnp.concatenate([c_ctx[None, :], c, jnp.zeros((N_COND - 1 - DEC_BATCH, D_MODEL), F32)], axis=0)
    mods4 = _mods(cond8, mod_w, mod_b).reshape(DEPTH, N_COND, 1, 6 * D_MODEL)
    gain_mix = norm_mix.reshape(DEPTH, 1, D_MODEL)
    gain_ffn = norm_ffn.reshape(DEPTH, 1, D_MODEL)
    router_w_pad = jnp.pad(router_w, ((0, 0), (0, 128 - N_EXPERTS)))
    n_rows_grid = DEC_SEQ // GRID_W
    rope_ret = _rope_tables(n_rows_grid, DK_B)
    rope_mla = _rope_tables(n_rows_grid, ROPE_DIM)
    zero_dn = jnp.zeros((BATCH, 2, H_A, DK_A, DV_A), F32)
    zero_ret = jnp.zeros((BATCH, 2, H_B, DK_B, DV_B), F32)
    new_dn, new_ret, new_ckv, new_krope = [], [], [], []

    for l in range(DEPTH):
        h = _modnorm(x, gain_mix, mods4, l, 0, 1)
        if l % 2 == 0:
            e = l // 2
            w = ev_w_in[e]
            w_main = jnp.concatenate([w[:, :3072], w[:, 3104:]], axis=1)
            w_ab = jnp.pad(w[:, 3072:3104], ((0, 0), (0, 96)))
            proj = _matmul(h, w_main, name="even_in")
            ab = _matmul(h, w_ab, tn=128, name="even_ab")[:, :32]
            args = (ev_conv[e], ev_a_log[e], ev_dt_bias[e], ev_dn_norm[e], ev_ret_log_decay[e], ev_ret_norm[e])
            yp, s_dn, s_ret = _pair_mixer(proj[:P_ROWS].reshape(BATCH, SEQ, -1), ab[:P_ROWS].reshape(BATCH, SEQ, 32),
                                          *args, zero_dn, zero_ret, None)
            ys, _, _ = _pair_mixer(proj[P_ROWS:].reshape(DEC_BATCH, DEC_SEQ, -1),
                                   ab[P_ROWS:].reshape(DEC_BATCH, DEC_SEQ, 32),
                                   *args, state_deltanet[:, e], state_retention[:, e], rope_ret)
            new_dn.append(s_dn)
            new_ret.append(s_ret)
            y_in = jnp.concatenate([yp.reshape(P_ROWS, -1), ys.reshape(S_ROWS, -1)], axis=0)
            x = _matmul_residual(y_in, ev_w_out, e, x, mods4, l, 2, "even_out")
        else:
            o = l // 2
            pr = _matmul(h, od_w_in, lead=o, tn=832, name="odd_in")
            cq = _rms(pr[:, :Q_RANK], od_q_norm[o])
            ckv = _rms(pr[:, Q_RANK:Q_RANK + KV_RANK], od_kv_norm[o])
            krope = pr[:, Q_RANK + KV_RANK:]
            wq = od_w_q_up[o].reshape(Q_RANK, H_C, NOPE_DIM + ROPE_DIM)
            wq = jnp.concatenate([wq[:, :, :NOPE_DIM].reshape(Q_RANK, -1), wq[:, :, NOPE_DIM:].reshape(Q_RANK, -1)], axis=1)
            q = _matmul(cq.astype(BF16), wq, name="odd_q_up")
            qn = q[:, :H_C * NOPE_DIM].astype(BF16)
            qr = q[:, H_C * NOPE_DIM:]
            ckv_p = ckv[:P_ROWS].reshape(BATCH, SEQ, KV_RANK)
            kr_p = krope[:P_ROWS].reshape(BATCH, SEQ, ROPE_DIM)
            new_ckv.append(ckv_p)
            new_krope.append(kr_p)
            cos, sin = rope_mla
            qr_s = _apply_rope(qr[P_ROWS:].reshape(DEC_BATCH, DEC_SEQ, H_C, ROPE_DIM), cos[:, None, :], sin[:, None, :])
            kr_s = _apply_rope(krope[P_ROWS:].reshape(DEC_BATCH, DEC_SEQ, ROPE_DIM), cos, sin)
            ckv_s = jnp.concatenate([cache_ckv[:, o], ckv[P_ROWS:].reshape(DEC_BATCH, DEC_SEQ, KV_RANK)], axis=1)
            kr_s = jnp.concatenate([cache_krope[:, o], kr_s], axis=1)
            ckv_all = jnp.concatenate([ckv_p.reshape(P_ROWS, KV_RANK), ckv_s.reshape(-1, KV_RANK)], axis=0)
            kv = _matmul(ckv_all.astype(BF16), od_w_kv_up, lead=o, out_dtype=BF16, name="odd_kv_up")
            kv_p = kv[:P_ROWS].reshape(BATCH, SEQ, -1)
            kv_s = kv[P_ROWS:].reshape(DEC_BATCH, PAST_LEN + DEC_SEQ, -1)
            ap = _attention(qn[:P_ROWS].reshape(BATCH, SEQ, -1), qr[:P_ROWS].astype(BF16).reshape(BATCH, SEQ, -1),
                            kv_p, kr_p.astype(BF16))
            as_ = _attention(qn[P_ROWS:].reshape(DEC_BATCH, DEC_SEQ, -1), qr_s.astype(BF16).reshape(DEC_BATCH, DEC_SEQ, -1),
                             kv_s, kr_s.astype(BF16))
            y_in = jnp.concatenate([ap.reshape(P_ROWS, -1), as_.reshape(S_ROWS, -1)], axis=0)
            x = _matmul_residual(y_in, od_w_out, o, x, mods4, l, 2, "odd_out")
        h2 = _modnorm(x, gain_ffn, mods4, l, 3, 4)
        y = _moe(h2, router_w_pad, router_b, moe_w1, moe_w3, moe_w2, l)
        gt2 = mods4[l, :, 0, 5 * D_MODEL:]
        cond_row = jnp.concatenate([jnp.zeros((P_ROWS,), jnp.int32),
                                    1 + jnp.arange(S_ROWS, dtype=jnp.int32) // DEC_SEQ])
        x = x + gt2[cond_row] * y

    yn = _final_norm(x, norm_final)
    return (yn[:P_ROWS].reshape(BATCH, SEQ, D_MODEL), yn[P_ROWS:].reshape(DEC_BATCH, DEC_SEQ, D_MODEL),
            jnp.stack(new_dn, axis=1), jnp.stack(new_ret, axis=1),
            jnp.stack(new_ckv, axis=1), jnp.stack(new_krope, axis=1))
```

```python
import functools
import math

import jax
import jax.numpy as jnp
from jax import lax
from jax.experimental import pallas as pl
from jax.experimental.pallas import tpu as pltpu

F32 = jnp.float32
BF16 = jnp.bfloat16

D_MODEL = 2048
BATCH = 32
SEQ = 256
DEPTH = 4
DEC_BATCH = 4
DEC_SEQ = 1024
PAST_LEN = 512
GRID_W = 64
EPS = 1e-6
H_A = 8
DK_A = 128
DV_A = 128
CONV_W = 5
H_B = 8
DK_B = 128
DV_B = 256
H_C = 16
Q_RANK = 512
KV_RANK = 256
NOPE_DIM = 128
ROPE_DIM = 64
V_DIM = 128
ROPE_THETA = 10000.0
N_EXPERTS = 16
N_GROUPS = 4
EXPERTS_PER_GROUP = N_EXPERTS // N_GROUPS
TOP_K = 2
D_EXPERT = 512

P_ROWS = BATCH * SEQ
S_ROWS = DEC_BATCH * DEC_SEQ
ROWS = P_ROWS + S_ROWS
N_COND = 8

LANES = 128
ROW_TILE = 512
VMEM_LIMIT = 48 * 1024 * 1024
CHUNK = 64
EPI_ROWS = 256
MOE_TILE = 256

EV_GATE_A = 3072
EV_Q_B = 4096
EV_K_B = 5120
EV_V_B = 6144
EV_GATE_B = 8192
EV_MAIN = 10240


def _cparams(sem):
    return pltpu.CompilerParams(dimension_semantics=sem, vmem_limit_bytes=VMEM_LIMIT)


def _cond_of_tile(i, tm):
    p_tiles = P_ROWS // tm
    per_req = DEC_SEQ // tm
    return jnp.where(i < p_tiles, 0, 1 + (i - p_tiles) // per_req)


def _mods_kernel(c_ref, w_ref, b_ref, o_ref):
    c = c_ref[...]
    s = (c * jax.nn.sigmoid(c)).astype(BF16)
    o_ref[0] = jnp.dot(s, w_ref[0].astype(BF16), preferred_element_type=F32) + b_ref[0]


def _mods(cond8, mod_w, mod_b):
    tn = 1024
    n = mod_w.shape[-1]
    return pl.pallas_call(
        _mods_kernel,
        grid=(DEPTH, n // tn),
        in_specs=[
            pl.BlockSpec((N_COND, D_MODEL), lambda l, j: (0, 0)),
            pl.BlockSpec((1, D_MODEL, tn), lambda l, j: (l, 0, j)),
            pl.BlockSpec((1, 1, tn), lambda l, j: (l, 0, j)),
        ],
        out_specs=pl.BlockSpec((1, N_COND, tn), lambda l, j: (l, 0, j)),
        out_shape=jax.ShapeDtypeStruct((DEPTH, N_COND, n), F32),
        compiler_params=_cparams(("arbitrary", "arbitrary")),
        name="adaln_mods",
    )(cond8, mod_w, mod_b.reshape(DEPTH, 1, n))


def _modnorm_kernel(x_ref, g_ref, sh_ref, sc_ref, o_ref):
    x = x_ref[...]
    y = x * lax.rsqrt(jnp.mean(x * x, axis=-1, keepdims=True) + EPS)
    y = y * g_ref[0]
    o_ref[...] = (y * (1.0 + sc_ref[0, 0]) + sh_ref[0, 0]).astype(o_ref.dtype)


def _modnorm(x, gain, mods4, l, seg_shift, seg_scale):
    tm = ROW_TILE
    return pl.pallas_call(
        _modnorm_kernel,
        grid=(ROWS // tm,),
        in_specs=[
            pl.BlockSpec((tm, D_MODEL), lambda i: (i, 0)),
            pl.BlockSpec((1, 1, D_MODEL), lambda i: (l, 0, 0)),
            pl.BlockSpec((1, 1, 1, D_MODEL), lambda i: (l, _cond_of_tile(i, tm), 0, seg_shift)),
            pl.BlockSpec((1, 1, 1, D_MODEL), lambda i: (l, _cond_of_tile(i, tm), 0, seg_scale)),
        ],
        out_specs=pl.BlockSpec((tm, D_MODEL), lambda i: (i, 0)),
        out_shape=jax.ShapeDtypeStruct((ROWS, D_MODEL), BF16),
        compiler_params=_cparams(("arbitrary",)),
        name="modnorm",
    )(x, gain, mods4, mods4)


def _rmsnorm_kernel(x_ref, g_ref, o_ref):
    x = x_ref[...]
    y = x * lax.rsqrt(jnp.mean(x * x, axis=-1, keepdims=True) + EPS)
    o_ref[...] = y * g_ref[...]


def _final_norm(x, gain):
    tm = ROW_TILE
    return pl.pallas_call(
        _rmsnorm_kernel,
        grid=(ROWS // tm,),
        in_specs=[pl.BlockSpec((tm, D_MODEL), lambda i: (i, 0)),
                  pl.BlockSpec((1, D_MODEL), lambda i: (0, 0))],
        out_specs=pl.BlockSpec((tm, D_MODEL), lambda i: (i, 0)),
        out_shape=jax.ShapeDtypeStruct((ROWS, D_MODEL), F32),
        compiler_params=_cparams(("arbitrary",)),
        name="final_norm",
    )(x, gain.reshape(1, D_MODEL))


def _load_b(b_ref):
    return b_ref[0] if len(b_ref.shape) == 3 else b_ref[...]


def _mm_kernel(a_ref, b_ref, o_ref, bscr):
    @pl.when(pl.program_id(1) == 0)
    def _():
        bscr[...] = _load_b(b_ref).astype(BF16)

    o_ref[...] = jnp.dot(a_ref[...], bscr[...], preferred_element_type=F32).astype(o_ref.dtype)


def _mm_res_kernel(a_ref, b_ref, x_ref, gt_ref, o_ref, bscr):
    @pl.when(pl.program_id(1) == 0)
    def _():
        bscr[...] = _load_b(b_ref).astype(BF16)

    y = jnp.dot(a_ref[...], bscr[...], preferred_element_type=F32)
    o_ref[...] = x_ref[...] + gt_ref[0, 0] * y


def _matmul(a, b, *, lead=None, out_dtype=F32, tm=ROW_TILE, tn=1024, name="matmul"):
    m, k = a.shape
    n = b.shape[-1]
    tn = min(tn, n)
    assert m % tm == 0 and n % tn == 0
    if lead is None:
        b_spec = pl.BlockSpec((k, tn), lambda j, i: (0, j))
    else:
        b_spec = pl.BlockSpec((1, k, tn), lambda j, i: (lead, 0, j))
    return pl.pallas_call(
        _mm_kernel,
        grid=(n // tn, m // tm),
        in_specs=[pl.BlockSpec((tm, k), lambda j, i: (i, 0)), b_spec],
        out_specs=pl.BlockSpec((tm, tn), lambda j, i: (i, j)),
        out_shape=jax.ShapeDtypeStruct((m, n), out_dtype),
        scratch_shapes=[pltpu.VMEM((k, tn), BF16)],
        compiler_params=_cparams(("arbitrary", "arbitrary")),
        name=name,
    )(a, b)


def _matmul_residual(a, b, lead, x, mods4, l, seg_gate, name):
    m, k = a.shape
    n = b.shape[-1]
    tm, tn = ROW_TILE, 1024
    return pl.pallas_call(
        _mm_res_kernel,
        grid=(n // tn, m // tm),
        in_specs=[
            pl.BlockSpec((tm, k), lambda j, i: (i, 0)),
            pl.BlockSpec((1, k, tn), lambda j, i: (lead, 0, j)),
            pl.BlockSpec((tm, tn), lambda j, i: (i, j)),
            pl.BlockSpec((1, 1, 1, tn),
                         lambda j, i: (l, _cond_of_tile(i, tm), 0, seg_gate * (D_MODEL // tn) + j)),
        ],
        out_specs=pl.BlockSpec((tm, tn), lambda j, i: (i, j)),
        out_shape=jax.ShapeDtypeStruct((m, n), F32),
        scratch_shapes=[pltpu.VMEM((k, tn), BF16)],
        compiler_params=_cparams(("arbitrary", "arbitrary")),
        name=name,
    )(a, b, x, mods4)


def _bf(x):
    return x.astype(BF16)


def _dot_nt(a, b):
    return lax.dot_general(_bf(a), _bf(b), (((1,), (1,)), ((), ())), preferred_element_type=F32)


def _dot_tn(a, b):
    return lax.dot_general(_bf(a), _bf(b), (((0,), (0,)), ((), ())), preferred_element_type=F32)


def _dot_sel(m01, x):
    hi = _bf(x)
    r1 = x - hi.astype(F32)
    mid = _bf(r1)
    lo = _bf(r1 - mid.astype(F32))
    mb = _bf(m01)
    acc = jnp.dot(mb, lo, preferred_element_type=F32)
    acc = acc + jnp.dot(mb, mid, preferred_element_type=F32)
    return acc + jnp.dot(mb, hi, preferred_element_type=F32)


def _split2(x):
    hi = _bf(x)
    return hi, _bf(x - hi.astype(F32))


def _dot_x3(a, b):
    ah, al = _split2(a)
    bh, bl = _split2(b)
    acc = jnp.dot(al, bh, preferred_element_type=F32)
    acc = acc + jnp.dot(ah, bl, preferred_element_type=F32)
    return acc + jnp.dot(ah, bh, preferred_element_type=F32)


def _chunk_masks(c, rev):
    row = lax.broadcasted_iota(jnp.int32, (c, c), 0)
    col = lax.broadcasted_iota(jnp.int32, (c, c), 1)
    incl = (row <= col) if rev else (row >= col)
    strict = (row < col) if rev else (row > col)
    incl_t = (row >= col) if rev else (row <= col)
    return row, col, incl, strict, incl_t


def _chunk_slice(ci):
    return pl.ds(pl.multiple_of(ci * CHUNK, CHUNK), CHUNK)


def _pick_lane(x, j):
    lane = lax.broadcasted_iota(jnp.int32, x.shape, 1)
    return jnp.sum(jnp.where(lane == j, x, 0.0), axis=1, keepdims=True)


def _delta_kernel(q_ref, k_ref, v_ref, gate_ref, gb_ref, s0_ref, nrm_ref, o_ref, sfin_ref,
                  u_scr, w_scr, a_scr, qd_scr, kd_scr, s_scr, of_scr, or_scr, *, n_chunks, hb, group):
    c = CHUNK
    n = n_chunks * c
    head0 = pl.program_id(1) * hb
    masks = [_chunk_masks(c, False), _chunk_masks(c, True)]
    ones_f = jnp.ones((c, c), F32)
    scale = DK_A ** -0.5

    def prep_chunk(d, h, ci):
        row, col, incl, strict, incl_t = masks[d]
        sl = _chunk_slice(ci)
        hs = slice(h * DK_A, (h + 1) * DK_A)
        q = q_ref[0, sl, hs]
        k = k_ref[0, sl, hs]
        v = v_ref[0, sl, hs]
        gall = gb_ref[0, sl, :]
        g = _pick_lane(gall, d * H_A + head0 + h)
        beta = _pick_lane(gall, (2 + d) * H_A + head0 + h)
        g_bc = jnp.broadcast_to(g, (c, c))
        gc_col = _dot_sel(incl.astype(F32), g_bc)
        gc_row = _dot_sel(ones_f, g_bc * incl_t.astype(F32))
        decay = jnp.where(incl, jnp.exp(jnp.where(incl, gc_col - gc_row, 0.0)), 0.0)
        gc1 = gc_col[:, :1]
        g_tot = jnp.sum(g, axis=0, keepdims=True)
        kb = k * beta
        p = -(_dot_nt(kb, k) * jnp.where(strict, decay, 0.0))
        t = (row == col).astype(F32) + p
        for _ in range(int(math.log2(c)) - 1):
            p = _dot_x3(p, p)
            t = t + _dot_x3(t, p)
        u = _dot_x3(t, v * beta)
        w = _dot_x3(t, kb * jnp.exp(gc1))
        qs = q * scale
        a = _dot_nt(qs, k) * decay
        idx = d * hb + h
        u_scr[idx, sl, :] = u
        w_scr[idx, sl, :] = _bf(w)
        a_scr[idx, sl, :] = _bf(a)
        qd_scr[idx, sl, :] = _bf(qs * jnp.exp(gc1))
        kd_scr[idx, sl, :] = _bf(k * jnp.exp(g_tot - gc1))

    def prep(cg, carry):
        for d in range(2):
            for h in range(hb):
                for j in range(group):
                    prep_chunk(d, h, cg * group + j)
        return carry

    lax.fori_loop(0, n_chunks // group, prep, 0)

    for d in range(2):
        for h in range(hb):
            s_scr[d * hb + h] = s0_ref[0, d, h]

    def scan(i, carry):
        for d in range(2):
            ci = (n_chunks - 1 - i) if d == 1 else i
            sl = _chunk_slice(ci)
            gall = gb_ref[0, sl, :]
            o_dst = or_scr if d == 1 else of_scr
            for h in range(hb):
                idx = d * hb + h
                g_tot = jnp.sum(_pick_lane(gall, d * H_A + head0 + h), axis=0, keepdims=True)
                s = s_scr[idx]
                sb = _bf(s)
                v_new = u_scr[idx, sl, :] - jnp.dot(w_scr[idx, sl, :], sb, preferred_element_type=F32)
                vb = _bf(v_new)
                o_dst[sl, h * DV_A:(h + 1) * DV_A] = (
                    jnp.dot(qd_scr[idx, sl, :], sb, preferred_element_type=F32)
                    + jnp.dot(a_scr[idx, sl, :], vb, preferred_element_type=F32))
                s_scr[idx] = s * jnp.exp(g_tot) + lax.dot_general(
                    kd_scr[idx, sl, :], vb, (((0,), (0,)), ((), ())), preferred_element_type=F32)
        return carry

    lax.fori_loop(0, n_chunks, scan, 0)

    for d in range(2):
        for h in range(hb):
            sfin_ref[0, d, h] = s_scr[d * hb + h]

    def epilogue(bi, carry):
        sl = pl.ds(pl.multiple_of(bi * EPI_ROWS, EPI_ROWS), EPI_ROWS)
        for h in range(hb):
            hs = slice(h * DV_A, (h + 1) * DV_A)
            o = of_scr[sl, hs] + or_scr[sl, hs]
            y = o * lax.rsqrt(jnp.mean(o * o, axis=-1, keepdims=True) + EPS) * nrm_ref[...]
            gt = gate_ref[0, sl, hs]
            o_ref[0, sl, hs] = (y * (gt * jax.nn.sigmoid(gt))).astype(o_ref.dtype)
        return carry

    lax.fori_loop(0, n // EPI_ROWS, epilogue, 0)


def _delta_mixer(q, k, v, proj, gb, s0, dn_norm, hb, group):
    b, n, _ = q.shape
    n_chunks = n // CHUNK
    w = hb * DK_A
    hd = lambda bi, hi: (bi, 0, hi)
    st = lambda bi, hi: (bi, 0, hi, 0, 0)
    return pl.pallas_call(
        functools.partial(_delta_kernel, n_chunks=n_chunks, hb=hb, group=group),
        grid=(b, H_A // hb),
        in_specs=[
            pl.BlockSpec((1, n, w), hd),
            pl.BlockSpec((1, n, w), hd),
            pl.BlockSpec((1, n, w), hd),
            pl.BlockSpec((1, n, w), lambda bi, hi: (bi, 0, EV_GATE_A // w + hi)),
            pl.BlockSpec((1, n, 4 * H_A), lambda bi, hi: (bi, 0, 0)),
            pl.BlockSpec((1, 2, hb, DK_A, DV_A), st),
            pl.BlockSpec((1, DV_A), lambda bi, hi: (0, 0)),
        ],
        out_specs=[pl.BlockSpec((1, n, w), hd), pl.BlockSpec((1, 2, hb, DK_A, DV_A), st)],
        out_shape=[jax.ShapeDtypeStruct((b, n, H_A * DV_A), BF16),
                   jax.ShapeDtypeStruct((b, 2, H_A, DK_A, DV_A), F32)],
        scratch_shapes=[
            pltpu.VMEM((2 * hb, n, DV_A), F32),
            pltpu.VMEM((2 * hb, n, DK_A), BF16),
            pltpu.VMEM((2 * hb, n, CHUNK), BF16),
            pltpu.VMEM((2 * hb, n, DK_A), BF16),
            pltpu.VMEM((2 * hb, n, DK_A), BF16),
            pltpu.VMEM((2 * hb, DK_A, DV_A), F32),
            pltpu.VMEM((n, w), F32),
            pltpu.VMEM((n, w), F32),
        ],
        compiler_params=_cparams(("arbitrary", "arbitrary")),
        name="delta_mixer",
    )(q, k, v, proj, gb, s0, dn_norm.reshape(1, DV_A))


def _ret_kernel(lg_ref, q_ref, k_ref, v_ref, gate_ref, s0_ref, nrm_ref, o_ref, sfin_ref,
                s_scr, of_scr, or_scr, *, n_chunks, hb):
    c = CHUNK
    n = n_chunks * c
    head0 = pl.program_id(1) * hb
    kscale = DK_B ** -0.5
    pos = lax.broadcasted_iota(jnp.int32, (c, 1), 0)

    for d in range(2):
        for h in range(hb):
            s_scr[d * hb + h] = s0_ref[0, d, h]

    def scan(i, carry):
        for d in range(2):
            rev = d == 1
            row, col, incl, _, _ = _chunk_masks(c, rev)
            dist = jnp.abs(row - col).astype(F32)
            steps = ((c - pos) if rev else (pos + 1)).astype(F32)
            ci = (n_chunks - 1 - i) if rev else i
            sl = _chunk_slice(ci)
            o_dst = or_scr if rev else of_scr
            for h in range(hb):
                idx = d * hb + h
                lg = jnp.full((1, 1), lg_ref[d, head0 + h], F32)
                decay = jnp.where(incl, jnp.exp(lg * dist), 0.0)
                q = q_ref[0, sl, h * DK_B:(h + 1) * DK_B]
                k = k_ref[0, sl, h * DK_B:(h + 1) * DK_B] * kscale
                vb = _bf(v_ref[0, sl, h * DV_B:(h + 1) * DV_B])
                a = _dot_nt(q, k) * decay
                s = s_scr[idx]
                o_dst[sl, h * DV_B:(h + 1) * DV_B] = (
                    jnp.dot(_bf(q * jnp.exp(lg * steps)), _bf(s), preferred_element_type=F32)
                    + jnp.dot(_bf(a), vb, preferred_element_type=F32))
                s_scr[idx] = s * jnp.exp(lg * c) + lax.dot_general(
                    _bf(k * jnp.exp(lg * (c - steps))), vb, (((0,), (0,)), ((), ())),
                    preferred_element_type=F32)
        return carry

    lax.fori_loop(0, n_chunks, scan, 0)

    for d in range(2):
        for h in range(hb):
            sfin_ref[0, d, h] = s_scr[d * hb + h]

    def epilogue(bi, carry):
        sl = pl.ds(pl.multiple_of(bi * EPI_ROWS, EPI_ROWS), EPI_ROWS)
        for h in range(hb):
            hs = slice(h * DV_B, (h + 1) * DV_B)
            r = of_scr[sl, hs] + or_scr[sl, hs]
            mu = jnp.mean(r, axis=-1, keepdims=True)
            rc = r - mu
            var = jnp.mean(rc * rc, axis=-1, keepdims=True)
            y = rc * lax.rsqrt(var + EPS) * nrm_ref[:, hs]
            gt = gate_ref[0, sl, hs]
            o_ref[0, sl, hs] = (y * (gt * jax.nn.sigmoid(gt))).astype(o_ref.dtype)
        return carry

    lax.fori_loop(0, n // EPI_ROWS, epilogue, 0)


def _ret_mixer(q, q_col, k, k_col, proj, log_decay, s0, ret_norm, hb):
    b, n, _ = proj.shape
    n_chunks = n // CHUNK
    wk = hb * DK_B
    wv = hb * DV_B
    st = lambda bi, hi: (bi, 0, hi, 0, 0)
    return pl.pallas_call(
        functools.partial(_ret_kernel, n_chunks=n_chunks, hb=hb),
        grid=(b, H_B // hb),
        in_specs=[
            pl.BlockSpec(memory_space=pltpu.SMEM),
            pl.BlockSpec((1, n, wk), lambda bi, hi: (bi, 0, q_col // wk + hi)),
            pl.BlockSpec((1, n, wk), lambda bi, hi: (bi, 0, k_col // wk + hi)),
            pl.BlockSpec((1, n, wv), lambda bi, hi: (bi, 0, EV_V_B // wv + hi)),
            pl.BlockSpec((1, n, wv), lambda bi, hi: (bi, 0, EV_GATE_B // wv + hi)),
            pl.BlockSpec((1, 2, hb, DK_B, DV_B), st),
            pl.BlockSpec((1, wv), lambda bi, hi: (0, hi)),
        ],
        out_specs=[pl.BlockSpec((1, n, wv), lambda bi, hi: (bi, 0, hi)),
                   pl.BlockSpec((1, 2, hb, DK_B, DV_B), st)],
        out_shape=[jax.ShapeDtypeStruct((b, n, H_B * DV_B), BF16),
                   jax.ShapeDtypeStruct((b, 2, H_B, DK_B, DV_B), F32)],
        scratch_shapes=[
            pltpu.VMEM((2 * hb, DK_B, DV_B), F32),
            pltpu.VMEM((n, wv), F32),
            pltpu.VMEM((n, wv), F32),
        ],
        compiler_params=_cparams(("arbitrary", "arbitrary")),
        name="ret_mixer",
    )(log_decay, q, k, proj, proj, s0, ret_norm.reshape(1, H_B * DV_B))


def _attn_kernel(qn_ref, qr_ref, kv_ref, kr_ref, o_ref):
    scale = (NOPE_DIM + ROPE_DIM) ** -0.5
    kr = kr_ref[0]
    for h in range(H_C):
        qn = qn_ref[0, :, h * NOPE_DIM:(h + 1) * NOPE_DIM]
        qr = qr_ref[0, :, h * ROPE_DIM:(h + 1) * ROPE_DIM]
        kn = kv_ref[0, :, h * 256:h * 256 + NOPE_DIM]
        vh = kv_ref[0, :, h * 256 + NOPE_DIM:(h + 1) * 256]
        s = (lax.dot_general(qn, kn, (((1,), (1,)), ((), ())), preferred_element_type=F32)
             + lax.dot_general(qr, kr, (((1,), (1,)), ((), ())), preferred_element_type=F32)) * scale
        m = jnp.max(s, axis=-1, keepdims=True)
        e = jnp.exp(s - m)
        p = e / jnp.sum(e, axis=-1, keepdims=True)
        o_ref[0, :, h * V_DIM:(h + 1) * V_DIM] = jnp.dot(
            _bf(p), vh, preferred_element_type=F32).astype(o_ref.dtype)


def _attention(qn, qr, kv, kr):
    b, nq, _ = qn.shape
    nk = kv.shape[1]
    tq = 256
    return pl.pallas_call(
        _attn_kernel,
        grid=(b, nq // tq),
        in_specs=[
            pl.BlockSpec((1, tq, H_C * NOPE_DIM), lambda bi, qi: (bi, qi, 0)),
            pl.BlockSpec((1, tq, H_C * ROPE_DIM), lambda bi, qi: (bi, qi, 0)),
            pl.BlockSpec((1, nk, H_C * 256), lambda bi, qi: (bi, 0, 0)),
            pl.BlockSpec((1, nk, ROPE_DIM), lambda bi, qi: (bi, 0, 0)),
        ],
        out_specs=pl.BlockSpec((1, tq, H_C * V_DIM), lambda bi, qi: (bi, qi, 0)),
        out_shape=jax.ShapeDtypeStruct((b, nq, H_C * V_DIM), BF16),
        compiler_params=_cparams(("arbitrary", "arbitrary")),
        name="mla_attention",
    )(qn, qr, kv, kr)


def _moe_kernel(te_ref, nv_ref, x_ref, w1_ref, w3_ref, w2_ref, o_ref, w1s, w3s, w2s):
    i = pl.program_id(0)
    prev = te_ref[jnp.maximum(i - 1, 0)]

    @pl.when((i == 0) | (te_ref[i] != prev))
    def _():
        w1s[...] = w1_ref[0, 0].astype(BF16)
        w3s[...] = w3_ref[0, 0].astype(BF16)
        w2s[...] = w2_ref[0, 0].astype(BF16)

    @pl.when(i < nv_ref[0])
    def _():
        x = x_ref[...]
        h1 = jnp.dot(x, w1s[...], preferred_element_type=F32)
        h3 = jnp.dot(x, w3s[...], preferred_element_type=F32)
        he = (h1 * jax.nn.sigmoid(h1)) * h3
        o_ref[...] = jnp.dot(_bf(he), w2s[...], preferred_element_type=F32)

    @pl.when(i >= nv_ref[0])
    def _():
        o_ref[...] = jnp.zeros_like(o_ref)


def _moe_experts(xg, tile_expert, n_valid, w1, w3, w2, l):
    r = xg.shape[0]
    tm = MOE_TILE
    wmap = lambda i, te, nv: (l, te[i], 0, 0)
    grid_spec = pltpu.PrefetchScalarGridSpec(
        num_scalar_prefetch=2,
        grid=(r // tm,),
        in_specs=[
            pl.BlockSpec((tm, D_MODEL), lambda i, te, nv: (i, 0)),
            pl.BlockSpec((1, 1, D_MODEL, D_EXPERT), wmap),
            pl.BlockSpec((1, 1, D_MODEL, D_EXPERT), wmap),
            pl.BlockSpec((1, 1, D_EXPERT, D_MODEL), wmap),
        ],
        out_specs=pl.BlockSpec((tm, D_MODEL), lambda i, te, nv: (i, 0)),
        scratch_shapes=[
            pltpu.VMEM((D_MODEL, D_EXPERT), BF16),
            pltpu.VMEM((D_MODEL, D_EXPERT), BF16),
            pltpu.VMEM((D_EXPERT, D_MODEL), BF16),
        ],
    )
    return pl.pallas_call(
        _moe_kernel,
        grid_spec=grid_spec,
        out_shape=jax.ShapeDtypeStruct((r, D_MODEL), F32),
        compiler_params=_cparams(("arbitrary",)),
        name="moe_experts",
    )(tile_expert, n_valid, xg, w1, w3, w2)


def _cumsum_kernel(x_ref, o_ref, carry):
    @pl.when(pl.program_id(0) == 0)
    def _():
        carry[...] = jnp.zeros_like(carry)

    tm = x_ref.shape[0]
    row = lax.broadcasted_iota(jnp.int32, (tm, tm), 0)
    col = lax.broadcasted_iota(jnp.int32, (tm, tm), 1)
    tri = jnp.where(row >= col, 1.0, 0.0).astype(BF16)
    c = jnp.dot(tri, x_ref[...], preferred_element_type=F32) + carry[...]
    o_ref[...] = c
    carry[...] = c[tm - 1:tm, :]


def _cumsum_rows(x):
    r, w = x.shape
    tm = ROW_TILE
    return pl.pallas_call(
        _cumsum_kernel,
        grid=(r // tm,),
        in_specs=[pl.BlockSpec((tm, w), lambda i: (i, 0))],
        out_specs=pl.BlockSpec((tm, w), lambda i: (i, 0)),
        out_shape=jax.ShapeDtypeStruct((r, w), F32),
        scratch_shapes=[pltpu.VMEM((1, w), F32)],
        compiler_params=_cparams(("arbitrary",)),
        name="expert_rank",
    )(x)


def _first_max(x):
    n = x.shape[-1]
    idx = jnp.arange(n, dtype=jnp.int32)
    m = jnp.max(x, axis=-1, keepdims=True)
    first = jnp.min(jnp.where(x == m, idx, n), axis=-1, keepdims=True)
    return first[..., 0], idx == first


def _route(logits, router_b):
    t = logits.shape[0]
    scores = jax.nn.sigmoid(logits)
    s4 = (scores + router_b.astype(F32)).reshape(t, N_GROUPS, EXPERTS_PER_GROUP)
    _, hot1 = _first_max(s4)
    m1 = jnp.max(s4, axis=-1)
    m2 = jnp.max(jnp.where(hot1, -jnp.inf, s4), axis=-1)
    gidx, ghot = _first_max(m1 + m2)
    sel_g = jnp.sum(jnp.where(ghot[:, :, None], s4, 0.0), axis=1)
    sc_g = jnp.sum(jnp.where(ghot[:, :, None], scores.reshape(t, N_GROUPS, EXPERTS_PER_GROUP), 0.0), axis=1)
    e1, h1 = _first_max(sel_g)
    e2, h2 = _first_max(jnp.where(h1, -jnp.inf, sel_g))
    w1 = jnp.sum(jnp.where(h1, sc_g, 0.0), axis=-1)
    w2 = jnp.sum(jnp.where(h2, sc_g, 0.0), axis=-1)
    eidx = gidx[:, None] * EXPERTS_PER_GROUP + jnp.stack([e1, e2], axis=1)
    wts = jnp.stack([w1, w2], axis=1)
    return eidx.astype(jnp.int32), wts / (w1 + w2)[:, None]


def _moe(h, router_w_pad, router_b, w1, w3, w2, l):
    t = h.shape[0]
    tm = MOE_TILE
    logits = _matmul(h, router_w_pad, tn=LANES, name="router")[:, :N_EXPERTS]
    eidx, wts = _route(logits, router_b)
    flat_e = eidx.reshape(-1)
    onehot = flat_e[:, None] == jnp.arange(LANES, dtype=jnp.int32)[None, :]
    csum = _cumsum_rows(onehot.astype(BF16))[:, :N_EXPERTS]
    oh = onehot[:, :N_EXPERTS]
    counts = csum[-1].astype(jnp.int32)
    padded = ((counts + tm - 1) // tm) * tm
    pad_end = jnp.cumsum(padded)
    pad_start = (pad_end - padded).astype(F32)
    dest = jnp.sum(jnp.where(oh, csum - 1.0 + pad_start[None, :], 0.0), axis=1).astype(jnp.int32)
    n_rows = 2 * t + N_EXPERTS * tm
    pair_tok = jnp.arange(2 * t, dtype=jnp.int32) // 2
    src_tok = jnp.zeros((n_rows,), jnp.int32).at[dest].set(pair_tok, unique_indices=True)
    n_tiles = n_rows // tm
    tile_start = jnp.arange(n_tiles, dtype=jnp.int32) * tm
    te = jnp.sum(tile_start[:, None] >= pad_end[None, :], axis=1)
    n_valid = (pad_end[-1] // tm).astype(jnp.int32)
    last_e = jnp.max(jnp.where(counts > 0, jnp.arange(N_EXPERTS), 0))
    te = jnp.where(jnp.arange(n_tiles) < n_valid, te, last_e).astype(jnp.int32)
    out = _moe_experts(h[src_tok], te, n_valid.reshape(1), w1, w3, w2, l)
    pos = dest.reshape(t, 2)
    return wts[:, 0:1] * out[pos[:, 0]] + wts[:, 1:2] * out[pos[:, 1]]


def _l2norm(x):
    return x * lax.rsqrt(jnp.sum(x * x, axis=-1, keepdims=True) + EPS)


def _rope_tables(n_rows, rot_dim):
    row = jnp.repeat(jnp.arange(n_rows, dtype=F32), GRID_W)
    col = jnp.tile(jnp.arange(GRID_W, dtype=F32), n_rows)
    n_freq = rot_dim // 4
    inv = jnp.power(ROPE_THETA, -jnp.arange(n_freq, dtype=F32) / n_freq)
    ang = jnp.concatenate([row[:, None] * inv, col[:, None] * inv], axis=-1)
    return jnp.cos(ang), jnp.sin(ang)


def _apply_rope(x, cos, sin):
    x1, x2 = jnp.split(x, 2, axis=-1)
    return jnp.concatenate([x1 * cos - x2 * sin, x2 * cos + x1 * sin], axis=-1)


def _dwconv(x, w):
    k, c = w.shape
    return lax.conv_general_dilated(x, w[:, None, :], window_strides=(1,), padding=[(k // 2, k // 2)],
                                    dimension_numbers=('NWC', 'WIO', 'NWC'), feature_group_count=c)


def _pair_mixer(proj, ab, conv_w, a_log, dt_bias, dn_norm, ret_log_decay, ret_norm, s_dn, s_ret, rope_tab,
                hb, group):
    b, n, _ = proj.shape
    qkv = jax.nn.silu(_dwconv(proj[..., :EV_GATE_A], conv_w))
    q_a = _l2norm(qkv[..., :1024].reshape(b, n, H_A, DK_A)).reshape(b, n, 1024)
    k_a = _l2norm(qkv[..., 1024:2048].reshape(b, n, H_A, DK_A)).reshape(b, n, 1024)
    v_a = qkv[..., 2048:3072]
    abr = ab.reshape(b, n, 2, 2, H_A)
    g = -jnp.exp(a_log) * jax.nn.softplus(abr[:, :, 0] + dt_bias)
    beta = jax.nn.sigmoid(abr[:, :, 1])
    gb = jnp.concatenate([g.reshape(b, n, 2 * H_A), beta.reshape(b, n, 2 * H_A)], axis=-1)
    o_a, s_dn_new = _delta_mixer(q_a, k_a, v_a, proj, gb, s_dn, dn_norm, hb, group)

    if rope_tab is None:
        o_r, s_ret_new = _ret_mixer(proj, EV_Q_B, proj, EV_K_B, proj, ret_log_decay, s_ret, ret_norm, hb)
    else:
        cos, sin = rope_tab
        rot = lambda t: _apply_rope(t.reshape(b, n, H_B, DK_B), cos[:, None, :], sin[:, None, :]).reshape(b, n, 1024)
        q_b = rot(proj[..., EV_Q_B:EV_K_B])
        k_b = rot(proj[..., EV_K_B:EV_V_B])
        o_r, s_ret_new = _ret_mixer(q_b, 0, k_b, 0, proj, ret_log_decay, s_ret, ret_norm, hb)
    return jnp.concatenate([o_a, o_r], axis=-1), s_dn_new, s_ret_new


def _rms(x, g):
    return x * lax.rsqrt(jnp.mean(x * x, axis=-1, keepdims=True) + EPS) * g


def kernel(x_prompt, x_sample, c, state_deltanet, state_retention, cache_ckv, cache_krope, c_ctx, mod_w, mod_b, norm_mix, norm_ffn, norm_final, ev_w_in, ev_conv, ev_a_log, ev_dt_bias, ev_dn_norm, ev_ret_log_decay, ev_ret_norm, ev_w_out, od_w_in, od_q_norm, od_kv_norm, od_w_q_up, od_w_kv_up, od_w_out, router_w, router_b, moe_w1, moe_w3, moe_w2):
    x = jnp.concatenate([x_prompt.reshape(P_ROWS, D_MODEL), x_sample.reshape(S_ROWS, D_MODEL)], axis=0)
    cond8 = jnp.concatenate([c_ctx[None, :], c, jnp.zeros((N_COND - 1 - DEC_BATCH, D_MODEL), F32)], axis=0)
    mods4 = _mods(cond8, mod_w, mod_b).reshape(DEPTH, N_COND, 1, 6 * D_MODEL)
    gain_mix = norm_mix.reshape(DEPTH, 1, D_MODEL)
    gain_ffn = norm_ffn.reshape(DEPTH, 1, D_MODEL)
    router_w_pad = jnp.pad(router_w, ((0, 0), (0, LANES - N_EXPERTS)))
    n_rows_grid = DEC_SEQ // GRID_W
    rope_ret = _rope_tables(n_rows_grid, DK_B)
    rope_mla = _rope_tables(n_rows_grid, ROPE_DIM)
    zero_dn = jnp.zeros((BATCH, 2, H_A, DK_A, DV_A), F32)
    zero_ret = jnp.zeros((BATCH, 2, H_B, DK_B, DV_B), F32)
    cond_row = jnp.concatenate([jnp.zeros((P_ROWS,), jnp.int32),
                                1 + jnp.arange(S_ROWS, dtype=jnp.int32) // DEC_SEQ])
    new_dn, new_ret, new_ckv, new_krope = [], [], [], []

    for l in range(DEPTH):
        h = _modnorm(x, gain_mix, mods4, l, 0, 1)
        if l % 2 == 0:
            e = l // 2
            w = ev_w_in[e]
            w_main = jnp.concatenate([w[:, :3072], w[:, 3104:]], axis=1)
            w_ab = jnp.pad(w[:, 3072:3104], ((0, 0), (0, LANES - 32)))
            proj = _matmul(h, w_main, name="even_in")
            ab = _matmul(h, w_ab, tn=LANES, name="even_ab")[:, :32]
            args = (ev_conv[e], ev_a_log[e], ev_dt_bias[e], ev_dn_norm[e], ev_ret_log_decay[e], ev_ret_norm[e])
            yp, s_dn, s_ret = _pair_mixer(proj[:P_ROWS].reshape(BATCH, SEQ, -1), ab[:P_ROWS].reshape(BATCH, SEQ, 32),
                                          *args, zero_dn, zero_ret, None, 4, 1)
            ys, _, _ = _pair_mixer(proj[P_ROWS:].reshape(DEC_BATCH, DEC_SEQ, -1),
                                   ab[P_ROWS:].reshape(DEC_BATCH, DEC_SEQ, 32),
                                   *args, state_deltanet[:, e], state_retention[:, e], rope_ret, 2, 2)
            new_dn.append(s_dn)
            new_ret.append(s_ret)
            y_in = jnp.concatenate([yp.reshape(P_ROWS, -1), ys.reshape(S_ROWS, -1)], axis=0)
            x = _matmul_residual(y_in, ev_w_out, e, x, mods4, l, 2, "even_out")
        else:
            o = l // 2
            pr = _matmul(h, od_w_in, lead=o, tn=832, name="odd_in")
            cq = _rms(pr[:, :Q_RANK], od_q_norm[o])
            ckv = _rms(pr[:, Q_RANK:Q_RANK + KV_RANK], od_kv_norm[o])
            krope = pr[:, Q_RANK + KV_RANK:]
            wq = od_w_q_up[o].reshape(Q_RANK, H_C, NOPE_DIM + ROPE_DIM)
            wq = jnp.concatenate([wq[:, :, :NOPE_DIM].reshape(Q_RANK, -1), wq[:, :, NOPE_DIM:].reshape(Q_RANK, -1)], axis=1)
            q = _matmul(cq.astype(BF16), wq, name="odd_q_up")
            qn = q[:, :H_C * NOPE_DIM].astype(BF16)
            qr = q[:, H_C * NOPE_DIM:]
            ckv_p = ckv[:P_ROWS].reshape(BATCH, SEQ, KV_RANK)
            kr_p = krope[:P_ROWS].reshape(BATCH, SEQ, ROPE_DIM)
            new_ckv.append(ckv_p)
            new_krope.append(kr_p)
            cos, sin = rope_mla
            qr_s = _apply_rope(qr[P_ROWS:].reshape(DEC_BATCH, DEC_SEQ, H_C, ROPE_DIM), cos[:, None, :], sin[:, None, :])
            kr_s = _apply_rope(krope[P_ROWS:].reshape(DEC_BATCH, DEC_SEQ, ROPE_DIM), cos, sin)
            ckv_s = jnp.concatenate([cache_ckv[:, o], ckv[P_ROWS:].reshape(DEC_BATCH, DEC_SEQ, KV_RANK)], axis=1)
            kr_s = jnp.concatenate([cache_krope[:, o], kr_s], axis=1)
            ckv_all = jnp.concatenate([ckv_p.reshape(P_ROWS, KV_RANK), ckv_s.reshape(-1, KV_RANK)], axis=0)
            kv = _matmul(ckv_all.astype(BF16), od_w_kv_up, lead=o, out_dtype=BF16, name="odd_kv_up")
            kv_p = kv[:P_ROWS].reshape(BATCH, SEQ, -1)
            kv_s = kv[P_ROWS:].reshape(DEC_BATCH, PAST_LEN + DEC_SEQ, -1)
            ap = _attention(qn[:P_ROWS].reshape(BATCH, SEQ, -1), qr[:P_ROWS].astype(BF16).reshape(BATCH, SEQ, -1),
                            kv_p, kr_p.astype(BF16))
            as_ = _attention(qn[P_ROWS:].reshape(DEC_BATCH, DEC_SEQ, -1), qr_s.astype(BF16).reshape(DEC_BATCH, DEC_SEQ, -1),
                             kv_s, kr_s.astype(BF16))
            y_in = jnp.concatenate([ap.reshape(P_ROWS, -1), as_.reshape(S_ROWS, -1)], axis=0)
            x = _matmul_residual(y_in, od_w_out, o, x, mods4, l, 2, "odd_out")
        h2 = _modnorm(x, gain_ffn, mods4, l, 3, 4)
        y = _moe(h2, router_w_pad, router_b, moe_w1, moe_w3, moe_w2, l)
        gt2 = mods4[l, :, 0, 5 * D_MODEL:]
        x = x + gt2[cond_row] * y

    yn = _final_norm(x, norm_final)
    return (yn[:P_ROWS].reshape(BATCH, SEQ, D_MODEL), yn[P_ROWS:].reshape(DEC_BATCH, DEC_SEQ, D_MODEL),
            jnp.stack(new_dn, axis=1), jnp.stack(new_ret, axis=1),
            jnp.stack(new_ckv, axis=1), jnp.stack(new_krope, axis=1))
```

```python
import functools
import math

import jax
import jax.numpy as jnp
from jax import lax
from jax.experimental import pallas as pl
from jax.experimental.pallas import tpu as pltpu

F32 = jnp.float32
BF16 = jnp.bfloat16

D_MODEL = 2048
BATCH = 32
SEQ = 256
DEPTH = 4
DEC_BATCH = 4
DEC_SEQ = 1024
PAST_LEN = 512
GRID_W = 64
EPS = 1e-6
H_A = 8
DK_A = 128
DV_A = 128
CONV_W = 5
H_B = 8
DK_B = 128
DV_B = 256
H_C = 16
Q_RANK = 512
KV_RANK = 256
NOPE_DIM = 128
ROPE_DIM = 64
V_DIM = 128
ROPE_THETA = 10000.0
N_EXPERTS = 16
N_GROUPS = 4
EXPERTS_PER_GROUP = N_EXPERTS // N_GROUPS
TOP_K = 2
D_EXPERT = 512

P_ROWS = BATCH * SEQ
S_ROWS = DEC_BATCH * DEC_SEQ
ROWS = P_ROWS + S_ROWS
N_COND = 8

LANES = 128
ROW_TILE = 512
VMEM_LIMIT = 48 * 1024 * 1024
CHUNK = 64
EPI_ROWS = 256
MOE_TILE = 256

EV_GATE_A = 3072
EV_Q_B = 4096
EV_K_B = 5120
EV_V_B = 6144
EV_GATE_B = 8192
EV_MAIN = 10240


def _cparams(sem):
    return pltpu.CompilerParams(dimension_semantics=sem, vmem_limit_bytes=VMEM_LIMIT)


def _cond_of_tile(i, tm):
    p_tiles = P_ROWS // tm
    per_req = DEC_SEQ // tm
    return jnp.where(i < p_tiles, 0, 1 + (i - p_tiles) // per_req)


def _mods_kernel(c_ref, w_ref, b_ref, o_ref):
    c = c_ref[...]
    s = (c * jax.nn.sigmoid(c)).astype(BF16)
    o_ref[0] = jnp.dot(s, w_ref[0].astype(BF16), preferred_element_type=F32) + b_ref[0]


def _mods(cond8, mod_w, mod_b):
    tn = 1024
    n = mod_w.shape[-1]
    return pl.pallas_call(
        _mods_kernel,
        grid=(DEPTH, n // tn),
        in_specs=[
            pl.BlockSpec((N_COND, D_MODEL), lambda l, j: (0, 0)),
            pl.BlockSpec((1, D_MODEL, tn), lambda l, j: (l, 0, j)),
            pl.BlockSpec((1, 1, tn), lambda l, j: (l, 0, j)),
        ],
        out_specs=pl.BlockSpec((1, N_COND, tn), lambda l, j: (l, 0, j)),
        out_shape=jax.ShapeDtypeStruct((DEPTH, N_COND, n), F32),
        compiler_params=_cparams(("arbitrary", "arbitrary")),
        name="adaln_mods",
    )(cond8, mod_w, mod_b.reshape(DEPTH, 1, n))


def _modnorm_kernel(x_ref, g_ref, sh_ref, sc_ref, o_ref):
    x = x_ref[...]
    y = x * lax.rsqrt(jnp.mean(x * x, axis=-1, keepdims=True) + EPS)
    y = y * g_ref[0]
    o_ref[...] = (y * (1.0 + sc_ref[0, 0]) + sh_ref[0, 0]).astype(o_ref.dtype)


def _modnorm(x, gain, mods4, l, seg_shift, seg_scale):
    tm = ROW_TILE
    return pl.pallas_call(
        _modnorm_kernel,
        grid=(ROWS // tm,),
        in_specs=[
            pl.BlockSpec((tm, D_MODEL), lambda i: (i, 0)),
            pl.BlockSpec((1, 1, D_MODEL), lambda i: (l, 0, 0)),
            pl.BlockSpec((1, 1, 1, D_MODEL), lambda i: (l, _cond_of_tile(i, tm), 0, seg_shift)),
            pl.BlockSpec((1, 1, 1, D_MODEL), lambda i: (l, _cond_of_tile(i, tm), 0, seg_scale)),
        ],
        out_specs=pl.BlockSpec((tm, D_MODEL), lambda i: (i, 0)),
        out_shape=jax.ShapeDtypeStruct((ROWS, D_MODEL), BF16),
        compiler_params=_cparams(("arbitrary",)),
        name="modnorm",
    )(x, gain, mods4, mods4)


def _rmsnorm_kernel(x_ref, g_ref, o_ref):
    x = x_ref[...]
    y = x * lax.rsqrt(jnp.mean(x * x, axis=-1, keepdims=True) + EPS)
    o_ref[...] = y * g_ref[...]


def _final_norm(x, gain):
    tm = ROW_TILE
    return pl.pallas_call(
        _rmsnorm_kernel,
        grid=(ROWS // tm,),
        in_specs=[pl.BlockSpec((tm, D_MODEL), lambda i: (i, 0)),
                  pl.BlockSpec((1, D_MODEL), lambda i: (0, 0))],
        out_specs=pl.BlockSpec((tm, D_MODEL), lambda i: (i, 0)),
        out_shape=jax.ShapeDtypeStruct((ROWS, D_MODEL), F32),
        compiler_params=_cparams(("arbitrary",)),
        name="final_norm",
    )(x, gain.reshape(1, D_MODEL))


def _load_b(b_ref):
    return b_ref[0] if len(b_ref.shape) == 3 else b_ref[...]


def _mm_kernel(a_ref, b_ref, o_ref, bscr):
    @pl.when(pl.program_id(1) == 0)
    def _():
        bscr[...] = _load_b(b_ref).astype(BF16)

    o_ref[...] = jnp.dot(a_ref[...], bscr[...], preferred_element_type=F32).astype(o_ref.dtype)


def _mm_res_kernel(a_ref, b_ref, x_ref, gt_ref, o_ref, bscr):
    @pl.when(pl.program_id(1) == 0)
    def _():
        bscr[...] = _load_b(b_ref).astype(BF16)

    y = jnp.dot(a_ref[...], bscr[...], preferred_element_type=F32)
    o_ref[...] = x_ref[...] + gt_ref[0, 0] * y


def _matmul(a, b, *, lead=None, out_dtype=F32, tm=ROW_TILE, tn=1024, name="matmul"):
    m, k = a.shape
    n = b.shape[-1]
    tn = min(tn, n)
    assert m % tm == 0 and n % tn == 0
    if lead is None:
        b_spec = pl.BlockSpec((k, tn), lambda j, i: (0, j))
    else:
        b_spec = pl.BlockSpec((1, k, tn), lambda j, i: (lead, 0, j))
    return pl.pallas_call(
        _mm_kernel,
        grid=(n // tn, m // tm),
        in_specs=[pl.BlockSpec((tm, k), lambda j, i: (i, 0)), b_spec],
        out_specs=pl.BlockSpec((tm, tn), lambda j, i: (i, j)),
        out_shape=jax.ShapeDtypeStruct((m, n), out_dtype),
        scratch_shapes=[pltpu.VMEM((k, tn), BF16)],
        compiler_params=_cparams(("arbitrary", "arbitrary")),
        name=name,
    )(a, b)


def _matmul_residual(a, b, lead, x, mods4, l, seg_gate, name):
    m, k = a.shape
    n = b.shape[-1]
    tm, tn = ROW_TILE, 1024
    return pl.pallas_call(
        _mm_res_kernel,
        grid=(n // tn, m // tm),
        in_specs=[
            pl.BlockSpec((tm, k), lambda j, i: (i, 0)),
            pl.BlockSpec((1, k, tn), lambda j, i: (lead, 0, j)),
            pl.BlockSpec((tm, tn), lambda j, i: (i, j)),
            pl.BlockSpec((1, 1, 1, tn),
                         lambda j, i: (l, _cond_of_tile(i, tm), 0, seg_gate * (D_MODEL // tn) + j)),
        ],
        out_specs=pl.BlockSpec((tm, tn), lambda j, i: (i, j)),
        out_shape=jax.ShapeDtypeStruct((m, n), F32),
        scratch_shapes=[pltpu.VMEM((k, tn), BF16)],
        compiler_params=_cparams(("arbitrary", "arbitrary")),
        name=name,
    )(a, b, x, mods4)


def _bf(x):
    return x.astype(BF16)


def _dot_nt(a, b):
    return lax.dot_general(_bf(a), _bf(b), (((1,), (1,)), ((), ())), preferred_element_type=F32)


def _dot_tn(a, b):
    return lax.dot_general(_bf(a), _bf(b), (((0,), (0,)), ((), ())), preferred_element_type=F32)


def _dot_sel(m01, x):
    hi = _bf(x)
    r1 = x - hi.astype(F32)
    mid = _bf(r1)
    lo = _bf(r1 - mid.astype(F32))
    mb = _bf(m01)
    acc = jnp.dot(mb, lo, preferred_element_type=F32)
    acc = acc + jnp.dot(mb, mid, preferred_element_type=F32)
    return acc + jnp.dot(mb, hi, preferred_element_type=F32)


def _split2(x):
    hi = _bf(x)
    return hi, _bf(x - hi.astype(F32))


def _dot_x3(a, b):
    ah, al = _split2(a)
    bh, bl = _split2(b)
    acc = jnp.dot(al, bh, preferred_element_type=F32)
    acc = acc + jnp.dot(ah, bl, preferred_element_type=F32)
    return acc + jnp.dot(ah, bh, preferred_element_type=F32)


def _chunk_masks(c, rev):
    row = lax.broadcasted_iota(jnp.int32, (c, c), 0)
    col = lax.broadcasted_iota(jnp.int32, (c, c), 1)
    incl = (row <= col) if rev else (row >= col)
    strict = (row < col) if rev else (row > col)
    incl_t = (row >= col) if rev else (row <= col)
    return row, col, incl, strict, incl_t


def _chunk_slice(ci):
    return pl.ds(pl.multiple_of(ci * CHUNK, CHUNK), CHUNK)


def _pick_lane(x, j):
    lane = lax.broadcasted_iota(jnp.int32, x.shape, 1)
    return jnp.sum(jnp.where(lane == j, x, 0.0), axis=1, keepdims=True)


def _delta_kernel(q_ref, k_ref, v_ref, gate_ref, gb_ref, s0_ref, nrm_ref, o_ref, sfin_ref,
                  u_scr, w_scr, a_scr, qd_scr, kd_scr, s_scr, of_scr, or_scr, *, n_chunks, hb, group):
    c = CHUNK
    n = n_chunks * c
    head0 = pl.program_id(1) * hb
    masks = [_chunk_masks(c, False), _chunk_masks(c, True)]
    ones_f = jnp.ones((c, c), F32)
    scale = DK_A ** -0.5

    def prep(cg, carry):
        chains = [(d, h, cg * group + j) for d in range(2) for h in range(hb) for j in range(group)]
        q, k, v, g, beta = [], [], [], [], []
        for d, h, ci in chains:
            sl = _chunk_slice(ci)
            hs = slice(h * DK_A, (h + 1) * DK_A)
            q.append(q_ref[0, sl, hs])
            k.append(k_ref[0, sl, hs])
            v.append(v_ref[0, sl, hs])
            gall = gb_ref[0, sl, :]
            g.append(_pick_lane(gall, d * H_A + head0 + h))
            beta.append(_pick_lane(gall, (2 + d) * H_A + head0 + h))
        nch = range(len(chains))
        g_bc = [jnp.broadcast_to(g[i], (c, c)) for i in nch]
        gc_col = [_dot_sel(masks[chains[i][0]][2].astype(F32), g_bc[i]) for i in nch]
        gc_row = [_dot_sel(ones_f, g_bc[i] * masks[chains[i][0]][4].astype(F32)) for i in nch]
        decay = []
        for i in nch:
            incl = masks[chains[i][0]][2]
            decay.append(jnp.where(incl, jnp.exp(jnp.where(incl, gc_col[i] - gc_row[i], 0.0)), 0.0))
        e_in = [jnp.exp(gc_col[i][:, :1]) for i in nch]
        g_tot = [jnp.sum(g[i], axis=0, keepdims=True) for i in nch]
        kb = [k[i] * beta[i] for i in nch]
        p = [-(_dot_nt(kb[i], k[i]) * jnp.where(masks[chains[i][0]][3], decay[i], 0.0)) for i in nch]
        eye = (masks[0][0] == masks[0][1]).astype(F32)
        t = [eye + p[i] for i in nch]
        for _ in range(int(math.log2(c)) - 1):
            p = [_dot_x3(p[i], p[i]) for i in nch]
            t = [t[i] + _dot_x3(t[i], p[i]) for i in nch]
        u = [_dot_x3(t[i], v[i] * beta[i]) for i in nch]
        w = [_dot_x3(t[i], kb[i] * e_in[i]) for i in nch]
        qs = [q[i] * scale for i in nch]
        a = [_dot_nt(qs[i], k[i]) * decay[i] for i in nch]
        for i in nch:
            d, h, ci = chains[i]
            sl = _chunk_slice(ci)
            idx = d * hb + h
            u_scr[idx, sl, :] = u[i]
            w_scr[idx, sl, :] = _bf(w[i])
            a_scr[idx, sl, :] = _bf(a[i])
            qd_scr[idx, sl, :] = _bf(qs[i] * e_in[i])
            kd_scr[idx, sl, :] = _bf(k[i] * jnp.exp(g_tot[i] - gc_col[i][:, :1]))
        return carry

    lax.fori_loop(0, n_chunks // group, prep, 0)

    for d in range(2):
        for h in range(hb):
            s_scr[d * hb + h] = s0_ref[0, d, h]

    def scan(i, carry):
        chains = [(d, h) for d in range(2) for h in range(hb)]
        sls = [_chunk_slice((n_chunks - 1 - i) if d == 1 else i) for d, h in chains]
        nch = range(len(chains))
        s = [s_scr[d * hb + h] for d, h in chains]
        sb = [_bf(x) for x in s]
        ws = [jnp.dot(w_scr[d * hb + h, sls[j], :], sb[j], preferred_element_type=F32)
              for j, (d, h) in enumerate(chains)]
        qsd = [jnp.dot(qd_scr[d * hb + h, sls[j], :], sb[j], preferred_element_type=F32)
               for j, (d, h) in enumerate(chains)]
        vb = [_bf(u_scr[d * hb + h, sls[j], :] - ws[j]) for j, (d, h) in enumerate(chains)]
        av = [jnp.dot(a_scr[d * hb + h, sls[j], :], vb[j], preferred_element_type=F32)
              for j, (d, h) in enumerate(chains)]
        kv = [lax.dot_general(kd_scr[d * hb + h, sls[j], :], vb[j], (((0,), (0,)), ((), ())),
                              preferred_element_type=F32) for j, (d, h) in enumerate(chains)]
        for j in nch:
            d, h = chains[j]
            gall = gb_ref[0, sls[j], :]
            g_tot = jnp.sum(_pick_lane(gall, d * H_A + head0 + h), axis=0, keepdims=True)
            o_dst = or_scr if d == 1 else of_scr
            o_dst[sls[j], h * DV_A:(h + 1) * DV_A] = qsd[j] + av[j]
            s_scr[d * hb + h] = s[j] * jnp.exp(g_tot) + kv[j]
        return carry

    lax.fori_loop(0, n_chunks, scan, 0)

    for d in range(2):
        for h in range(hb):
            sfin_ref[0, d, h] = s_scr[d * hb + h]

    def epilogue(bi, carry):
        sl = pl.ds(pl.multiple_of(bi * EPI_ROWS, EPI_ROWS), EPI_ROWS)
        for h in range(hb):
            hs = slice(h * DV_A, (h + 1) * DV_A)
            o = of_scr[sl, hs] + or_scr[sl, hs]
            y = o * lax.rsqrt(jnp.mean(o * o, axis=-1, keepdims=True) + EPS) * nrm_ref[...]
            gt = gate_ref[0, sl, hs]
            o_ref[0, sl, hs] = (y * (gt * jax.nn.sigmoid(gt))).astype(o_ref.dtype)
        return carry

    lax.fori_loop(0, n // EPI_ROWS, epilogue, 0)


def _delta_mixer(q, k, v, proj, gb, s0, dn_norm, hb, group):
    b, n, _ = q.shape
    n_chunks = n // CHUNK
    w = hb * DK_A
    hd = lambda bi, hi: (bi, 0, hi)
    st = lambda bi, hi: (bi, 0, hi, 0, 0)
    return pl.pallas_call(
        functools.partial(_delta_kernel, n_chunks=n_chunks, hb=hb, group=group),
        grid=(b, H_A // hb),
        in_specs=[
            pl.BlockSpec((1, n, w), hd),
            pl.BlockSpec((1, n, w), hd),
            pl.BlockSpec((1, n, w), hd),
            pl.BlockSpec((1, n, w), lambda bi, hi: (bi, 0, EV_GATE_A // w + hi)),
            pl.BlockSpec((1, n, 4 * H_A), lambda bi, hi: (bi, 0, 0)),
            pl.BlockSpec((1, 2, hb, DK_A, DV_A), st),
            pl.BlockSpec((1, DV_A), lambda bi, hi: (0, 0)),
        ],
        out_specs=[pl.BlockSpec((1, n, w), hd), pl.BlockSpec((1, 2, hb, DK_A, DV_A), st)],
        out_shape=[jax.ShapeDtypeStruct((b, n, H_A * DV_A), BF16),
                   jax.ShapeDtypeStruct((b, 2, H_A, DK_A, DV_A), F32)],
        scratch_shapes=[
            pltpu.VMEM((2 * hb, n, DV_A), F32),
            pltpu.VMEM((2 * hb, n, DK_A), BF16),
            pltpu.VMEM((2 * hb, n, CHUNK), BF16),
            pltpu.VMEM((2 * hb, n, DK_A), BF16),
            pltpu.VMEM((2 * hb, n, DK_A), BF16),
            pltpu.VMEM((2 * hb, DK_A, DV_A), F32),
            pltpu.VMEM((n, w), F32),
            pltpu.VMEM((n, w), F32),
        ],
        compiler_params=_cparams(("arbitrary", "arbitrary")),
        name="delta_mixer",
    )(q, k, v, proj, gb, s0, dn_norm.reshape(1, DV_A))


def _ret_kernel(lg_ref, q_ref, k_ref, v_ref, gate_ref, s0_ref, nrm_ref, o_ref, sfin_ref,
                s_scr, of_scr, or_scr, *, n_chunks, hb):
    c = CHUNK
    n = n_chunks * c
    head0 = pl.program_id(1) * hb
    kscale = DK_B ** -0.5
    pos = lax.broadcasted_iota(jnp.int32, (c, 1), 0)

    for d in range(2):
        for h in range(hb):
            s_scr[d * hb + h] = s0_ref[0, d, h]

    def scan(i, carry):
        chains = [(d, h) for d in range(2) for h in range(hb)]
        nch = range(len(chains))
        masks = [_chunk_masks(c, False), _chunk_masks(c, True)]
        dist = jnp.abs(masks[0][0] - masks[0][1]).astype(F32)
        steps = [(pos + 1).astype(F32), (c - pos).astype(F32)]
        sls = [_chunk_slice((n_chunks - 1 - i) if d == 1 else i) for d, h in chains]
        lg = [jnp.full((1, 1), lg_ref[d, head0 + h], F32) for d, h in chains]
        q = [q_ref[0, sls[j], h * DK_B:(h + 1) * DK_B] for j, (d, h) in enumerate(chains)]
        k = [k_ref[0, sls[j], h * DK_B:(h + 1) * DK_B] * kscale for j, (d, h) in enumerate(chains)]
        vb = [_bf(v_ref[0, sls[j], h * DV_B:(h + 1) * DV_B]) for j, (d, h) in enumerate(chains)]
        s = [s_scr[d * hb + h] for d, h in chains]
        qk = [_dot_nt(q[j], k[j]) for j in nch]
        qs = [jnp.dot(_bf(q[j] * jnp.exp(lg[j] * steps[chains[j][0]])), _bf(s[j]),
                      preferred_element_type=F32) for j in nch]
        kv = [lax.dot_general(_bf(k[j] * jnp.exp(lg[j] * (c - steps[chains[j][0]]))), vb[j],
                              (((0,), (0,)), ((), ())), preferred_element_type=F32) for j in nch]
        a = [_bf(qk[j] * jnp.where(masks[chains[j][0]][2], jnp.exp(lg[j] * dist), 0.0)) for j in nch]
        av = [jnp.dot(a[j], vb[j], preferred_element_type=F32) for j in nch]
        for j in nch:
            d, h = chains[j]
            o_dst = or_scr if d == 1 else of_scr
            o_dst[sls[j], h * DV_B:(h + 1) * DV_B] = qs[j] + av[j]
            s_scr[d * hb + h] = s[j] * jnp.exp(lg[j] * c) + kv[j]
        return carry

    lax.fori_loop(0, n_chunks, scan, 0)

    for d in range(2):
        for h in range(hb):
            sfin_ref[0, d, h] = s_scr[d * hb + h]

    def epilogue(bi, carry):
        sl = pl.ds(pl.multiple_of(bi * EPI_ROWS, EPI_ROWS), EPI_ROWS)
        for h in range(hb):
            hs = slice(h * DV_B, (h + 1) * DV_B)
            r = of_scr[sl, hs] + or_scr[sl, hs]
            mu = jnp.mean(r, axis=-1, keepdims=True)
            rc = r - mu
            var = jnp.mean(rc * rc, axis=-1, keepdims=True)
            y = rc * lax.rsqrt(var + EPS) * nrm_ref[:, hs]
            gt = gate_ref[0, sl, hs]
            o_ref[0, sl, hs] = (y * (gt * jax.nn.sigmoid(gt))).astype(o_ref.dtype)
        return carry

    lax.fori_loop(0, n // EPI_ROWS, epilogue, 0)


def _ret_mixer(q, q_col, k, k_col, proj, log_decay, s0, ret_norm, hb):
    b, n, _ = proj.shape
    n_chunks = n // CHUNK
    wk = hb * DK_B
    wv = hb * DV_B
    st = lambda bi, hi: (bi, 0, hi, 0, 0)
    return pl.pallas_call(
        functools.partial(_ret_kernel, n_chunks=n_chunks, hb=hb),
        grid=(b, H_B // hb),
        in_specs=[
            pl.BlockSpec(memory_space=pltpu.SMEM),
            pl.BlockSpec((1, n, wk), lambda bi, hi: (bi, 0, q_col // wk + hi)),
            pl.BlockSpec((1, n, wk), lambda bi, hi: (bi, 0, k_col // wk + hi)),
            pl.BlockSpec((1, n, wv), lambda bi, hi: (bi, 0, EV_V_B // wv + hi)),
            pl.BlockSpec((1, n, wv), lambda bi, hi: (bi, 0, EV_GATE_B // wv + hi)),
            pl.BlockSpec((1, 2, hb, DK_B, DV_B), st),
            pl.BlockSpec((1, wv), lambda bi, hi: (0, hi)),
        ],
        out_specs=[pl.BlockSpec((1, n, wv), lambda bi, hi: (bi, 0, hi)),
                   pl.BlockSpec((1, 2, hb, DK_B, DV_B), st)],
        out_shape=[jax.ShapeDtypeStruct((b, n, H_B * DV_B), BF16),
                   jax.ShapeDtypeStruct((b, 2, H_B, DK_B, DV_B), F32)],
        scratch_shapes=[
            pltpu.VMEM((2 * hb, DK_B, DV_B), F32),
            pltpu.VMEM((n, wv), F32),
            pltpu.VMEM((n, wv), F32),
        ],
        compiler_params=_cparams(("arbitrary", "arbitrary")),
        name="ret_mixer",
    )(log_decay, q, k, proj, proj, s0, ret_norm.reshape(1, H_B * DV_B))


def _attn_kernel(qn_ref, qr_ref, kv_ref, kr_ref, o_ref):
    scale = (NOPE_DIM + ROPE_DIM) ** -0.5
    kr = kr_ref[0]

    def scores(h):
        qn = qn_ref[0, :, h * NOPE_DIM:(h + 1) * NOPE_DIM]
        qr = qr_ref[0, :, h * ROPE_DIM:(h + 1) * ROPE_DIM]
        kn = kv_ref[0, :, h * 256:h * 256 + NOPE_DIM]
        return (lax.dot_general(qn, kn, (((1,), (1,)), ((), ())), preferred_element_type=F32)
                + lax.dot_general(qr, kr, (((1,), (1,)), ((), ())), preferred_element_type=F32)) * scale

    s_next = scores(0)
    for h in range(H_C):
        s = s_next
        if h + 1 < H_C:
            s_next = scores(h + 1)
        vh = kv_ref[0, :, h * 256 + NOPE_DIM:(h + 1) * 256]
        m = jnp.max(s, axis=-1, keepdims=True)
        e = jnp.exp(s - m)
        p = e / jnp.sum(e, axis=-1, keepdims=True)
        o_ref[0, :, h * V_DIM:(h + 1) * V_DIM] = jnp.dot(
            _bf(p), vh, preferred_element_type=F32).astype(o_ref.dtype)


def _attention(qn, qr, kv, kr):
    b, nq, _ = qn.shape
    nk = kv.shape[1]
    tq = 256
    return pl.pallas_call(
        _attn_kernel,
        grid=(b, nq // tq),
        in_specs=[
            pl.BlockSpec((1, tq, H_C * NOPE_DIM), lambda bi, qi: (bi, qi, 0)),
            pl.BlockSpec((1, tq, H_C * ROPE_DIM), lambda bi, qi: (bi, qi, 0)),
            pl.BlockSpec((1, nk, H_C * 256), lambda bi, qi: (bi, 0, 0)),
            pl.BlockSpec((1, nk, ROPE_DIM), lambda bi, qi: (bi, 0, 0)),
        ],
        out_specs=pl.BlockSpec((1, tq, H_C * V_DIM), lambda bi, qi: (bi, qi, 0)),
        out_shape=jax.ShapeDtypeStruct((b, nq, H_C * V_DIM), BF16),
        compiler_params=_cparams(("arbitrary", "arbitrary")),
        name="mla_attention",
    )(qn, qr, kv, kr)


def _moe_kernel(te_ref, nv_ref, x_ref, w1_ref, w3_ref, w2_ref, o_ref, w1s, w3s, w2s):
    i = pl.program_id(0)
    prev = te_ref[jnp.maximum(i - 1, 0)]

    @pl.when((i == 0) | (te_ref[i] != prev))
    def _():
        w1s[...] = w1_ref[0, 0].astype(BF16)
        w3s[...] = w3_ref[0, 0].astype(BF16)
        w2s[...] = w2_ref[0, 0].astype(BF16)

    @pl.when(i < nv_ref[0])
    def _():
        x = x_ref[...]
        h1 = jnp.dot(x, w1s[...], preferred_element_type=F32)
        h3 = jnp.dot(x, w3s[...], preferred_element_type=F32)
        he = (h1 * jax.nn.sigmoid(h1)) * h3
        o_ref[...] = jnp.dot(_bf(he), w2s[...], preferred_element_type=F32)

    @pl.when(i >= nv_ref[0])
    def _():
        o_ref[...] = jnp.zeros_like(o_ref)


def _moe_experts(xg, tile_expert, n_valid, w1, w3, w2, l):
    r = xg.shape[0]
    tm = MOE_TILE
    wmap = lambda i, te, nv: (l, te[i], 0, 0)
    grid_spec = pltpu.PrefetchScalarGridSpec(
        num_scalar_prefetch=2,
        grid=(r // tm,),
        in_specs=[
            pl.BlockSpec((tm, D_MODEL), lambda i, te, nv: (i, 0)),
            pl.BlockSpec((1, 1, D_MODEL, D_EXPERT), wmap),
            pl.BlockSpec((1, 1, D_MODEL, D_EXPERT), wmap),
            pl.BlockSpec((1, 1, D_EXPERT, D_MODEL), wmap),
        ],
        out_specs=pl.BlockSpec((tm, D_MODEL), lambda i, te, nv: (i, 0)),
        scratch_shapes=[
            pltpu.VMEM((D_MODEL, D_EXPERT), BF16),
            pltpu.VMEM((D_MODEL, D_EXPERT), BF16),
            pltpu.VMEM((D_EXPERT, D_MODEL), BF16),
        ],
    )
    return pl.pallas_call(
        _moe_kernel,
        grid_spec=grid_spec,
        out_shape=jax.ShapeDtypeStruct((r, D_MODEL), F32),
        compiler_params=_cparams(("arbitrary",)),
        name="moe_experts",
    )(tile_expert, n_valid, xg, w1, w3, w2)


def _cumsum_kernel(x_ref, o_ref, carry):
    @pl.when(pl.program_id(0) == 0)
    def _():
        carry[...] = jnp.zeros_like(carry)

    tm = x_ref.shape[0]
    row = lax.broadcasted_iota(jnp.int32, (tm, tm), 0)
    col = lax.broadcasted_iota(jnp.int32, (tm, tm), 1)
    tri = jnp.where(row >= col, 1.0, 0.0).astype(BF16)
    c = jnp.dot(tri, x_ref[...], preferred_element_type=F32) + carry[...]
    o_ref[...] = c
    carry[...] = c[tm - 1:tm, :]


def _cumsum_rows(x):
    r, w = x.shape
    tm = ROW_TILE
    return pl.pallas_call(
        _cumsum_kernel,
        grid=(r // tm,),
        in_specs=[pl.BlockSpec((tm, w), lambda i: (i, 0))],
        out_specs=pl.BlockSpec((tm, w), lambda i: (i, 0)),
        out_shape=jax.ShapeDtypeStruct((r, w), F32),
        scratch_shapes=[pltpu.VMEM((1, w), F32)],
        compiler_params=_cparams(("arbitrary",)),
        name="expert_rank",
    )(x)


def _first_max(x):
    n = x.shape[-1]
    idx = jnp.arange(n, dtype=jnp.int32)
    m = jnp.max(x, axis=-1, keepdims=True)
    first = jnp.min(jnp.where(x == m, idx, n), axis=-1, keepdims=True)
    return first[..., 0], idx == first


def _route(logits, router_b):
    t = logits.shape[0]
    scores = jax.nn.sigmoid(logits)
    s4 = (scores + router_b.astype(F32)).reshape(t, N_GROUPS, EXPERTS_PER_GROUP)
    _, hot1 = _first_max(s4)
    m1 = jnp.max(s4, axis=-1)
    m2 = jnp.max(jnp.where(hot1, -jnp.inf, s4), axis=-1)
    gidx, ghot = _first_max(m1 + m2)
    sel_g = jnp.sum(jnp.where(ghot[:, :, None], s4, 0.0), axis=1)
    sc_g = jnp.sum(jnp.where(ghot[:, :, None], scores.reshape(t, N_GROUPS, EXPERTS_PER_GROUP), 0.0), axis=1)
    e1, h1 = _first_max(sel_g)
    e2, h2 = _first_max(jnp.where(h1, -jnp.inf, sel_g))
    w1 = jnp.sum(jnp.where(h1, sc_g, 0.0), axis=-1)
    w2 = jnp.sum(jnp.where(h2, sc_g, 0.0), axis=-1)
    eidx = gidx[:, None] * EXPERTS_PER_GROUP + jnp.stack([e1, e2], axis=1)
    wts = jnp.stack([w1, w2], axis=1)
    return eidx.astype(jnp.int32), wts / (w1 + w2)[:, None]


def _moe(h, router_w_pad, router_b, w1, w3, w2, l):
    t = h.shape[0]
    tm = MOE_TILE
    logits = _matmul(h, router_w_pad, tn=LANES, name="router")[:, :N_EXPERTS]
    eidx, wts = _route(logits, router_b)
    flat_e = eidx.reshape(-1)
    onehot = flat_e[:, None] == jnp.arange(LANES, dtype=jnp.int32)[None, :]
    csum = _cumsum_rows(onehot.astype(BF16))[:, :N_EXPERTS]
    oh = onehot[:, :N_EXPERTS]
    counts = csum[-1].astype(jnp.int32)
    padded = ((counts + tm - 1) // tm) * tm
    pad_end = jnp.cumsum(padded)
    pad_start = (pad_end - padded).astype(F32)
    dest = jnp.sum(jnp.where(oh, csum - 1.0 + pad_start[None, :], 0.0), axis=1).astype(jnp.int32)
    n_rows = 2 * t + N_EXPERTS * tm
    pair_tok = jnp.arange(2 * t, dtype=jnp.int32) // 2
    src_tok = jnp.zeros((n_rows,), jnp.int32).at[dest].set(pair_tok, unique_indices=True)
    n_tiles = n_rows // tm
    tile_start = jnp.arange(n_tiles, dtype=jnp.int32) * tm
    te = jnp.sum(tile_start[:, None] >= pad_end[None, :], axis=1)
    n_valid = (pad_end[-1] // tm).astype(jnp.int32)
    last_e = jnp.max(jnp.where(counts > 0, jnp.arange(N_EXPERTS), 0))
    te = jnp.where(jnp.arange(n_tiles) < n_valid, te, last_e).astype(jnp.int32)
    out = _moe_experts(h[src_tok], te, n_valid.reshape(1), w1, w3, w2, l)
    pos = dest.reshape(t, 2)
    return wts[:, 0:1] * out[pos[:, 0]] + wts[:, 1:2] * out[pos[:, 1]]


def _l2norm(x):
    return x * lax.rsqrt(jnp.sum(x * x, axis=-1, keepdims=True) + EPS)


def _rope_tables(n_rows, rot_dim):
    row = jnp.repeat(jnp.arange(n_rows, dtype=F32), GRID_W)
    col = jnp.tile(jnp.arange(GRID_W, dtype=F32), n_rows)
    n_freq = rot_dim // 4
    inv = jnp.power(ROPE_THETA, -jnp.arange(n_freq, dtype=F32) / n_freq)
    ang = jnp.concatenate([row[:, None] * inv, col[:, None] * inv], axis=-1)
    return jnp.cos(ang), jnp.sin(ang)


def _apply_rope(x, cos, sin):
    x1, x2 = jnp.split(x, 2, axis=-1)
    return jnp.concatenate([x1 * cos - x2 * sin, x2 * cos + x1 * sin], axis=-1)


def _dwconv(x, w):
    k, c = w.shape
    return lax.conv_general_dilated(x, w[:, None, :], window_strides=(1,), padding=[(k // 2, k // 2)],
                                    dimension_numbers=('NWC', 'WIO', 'NWC'), feature_group_count=c)


def _pair_mixer(proj, ab, conv_w, a_log, dt_bias, dn_norm, ret_log_decay, ret_norm, s_dn, s_ret, rope_tab,
                hb, group):
    b, n, _ = proj.shape
    qkv = jax.nn.silu(_dwconv(proj[..., :EV_GATE_A], conv_w))
    q_a = _l2norm(qkv[..., :1024].reshape(b, n, H_A, DK_A)).reshape(b, n, 1024)
    k_a = _l2norm(qkv[..., 1024:2048].reshape(b, n, H_A, DK_A)).reshape(b, n, 1024)
    v_a = qkv[..., 2048:3072]
    abr = ab.reshape(b, n, 2, 2, H_A)
    g = -jnp.exp(a_log) * jax.nn.softplus(abr[:, :, 0] + dt_bias)
    beta = jax.nn.sigmoid(abr[:, :, 1])
    gb = jnp.concatenate([g.reshape(b, n, 2 * H_A), beta.reshape(b, n, 2 * H_A)], axis=-1)
    o_a, s_dn_new = _delta_mixer(q_a, k_a, v_a, proj, gb, s_dn, dn_norm, hb, group)

    if rope_tab is None:
        o_r, s_ret_new = _ret_mixer(proj, EV_Q_B, proj, EV_K_B, proj, ret_log_decay, s_ret, ret_norm, hb)
    else:
        cos, sin = rope_tab
        rot = lambda t: _apply_rope(t.reshape(b, n, H_B, DK_B), cos[:, None, :], sin[:, None, :]).reshape(b, n, 1024)
        q_b = rot(proj[..., EV_Q_B:EV_K_B])
        k_b = rot(proj[..., EV_K_B:EV_V_B])
        o_r, s_ret_new = _ret_mixer(q_b, 0, k_b, 0, proj, ret_log_decay, s_ret, ret_norm, hb)
    return jnp.concatenate([o_a, o_r], axis=-1), s_dn_new, s_ret_new


def _rms(x, g):
    return x * lax.rsqrt(jnp.mean(x * x, axis=-1, keepdims=True) + EPS) * g


def kernel(x_prompt, x_sample, c, state_deltanet, state_retention, cache_ckv, cache_krope, c_ctx, mod_w, mod_b, norm_mix, norm_ffn, norm_final, ev_w_in, ev_conv, ev_a_log, ev_dt_bias, ev_dn_norm, ev_ret_log_decay, ev_ret_norm, ev_w_out, od_w_in, od_q_norm, od_kv_norm, od_w_q_up, od_w_kv_up, od_w_out, router_w, router_b, moe_w1, moe_w3, moe_w2):
    x = jnp.concatenate([x_prompt.reshape(P_ROWS, D_MODEL), x_sample.reshape(S_ROWS, D_MODEL)], axis=0)
    cond8 = jnp.concatenate([c_ctx[None, :], c, jnp.zeros((N_COND - 1 - DEC_BATCH, D_MODEL), F32)], axis=0)
    mods4 = _mods(cond8, mod_w, mod_b).reshape(DEPTH, N_COND, 1, 6 * D_MODEL)
    gain_mix = norm_mix.reshape(DEPTH, 1, D_MODEL)
    gain_ffn = norm_ffn.reshape(DEPTH, 1, D_MODEL)
    router_w_pad = jnp.pad(router_w, ((0, 0), (0, LANES - N_EXPERTS)))
    n_rows_grid = DEC_SEQ // GRID_W
    rope_ret = _rope_tables(n_rows_grid, DK_B)
    rope_mla = _rope_tables(n_rows_grid, ROPE_DIM)
    zero_dn = jnp.zeros((BATCH, 2, H_A, DK_A, DV_A), F32)
    zero_ret = jnp.zeros((BATCH, 2, H_B, DK_B, DV_B), F32)
    cond_row = jnp.concatenate([jnp.zeros((P_ROWS,), jnp.int32),
                                1 + jnp.arange(S_ROWS, dtype=jnp.int32) // DEC_SEQ])
    new_dn, new_ret, new_ckv, new_krope = [], [], [], []

    for l in range(DEPTH):
        h = _modnorm(x, gain_mix, mods4, l, 0, 1)
        if l % 2 == 0:
            e = l // 2
            w = ev_w_in[e]
            w_main = jnp.concatenate([w[:, :3072], w[:, 3104:]], axis=1)
            w_ab = jnp.pad(w[:, 3072:3104], ((0, 0), (0, LANES - 32)))
            proj = _matmul(h, w_main, name="even_in")
            ab = _matmul(h, w_ab, tn=LANES, name="even_ab")[:, :32]
            args = (ev_conv[e], ev_a_log[e], ev_dt_bias[e], ev_dn_norm[e], ev_ret_log_decay[e], ev_ret_norm[e])
            yp, s_dn, s_ret = _pair_mixer(proj[:P_ROWS].reshape(BATCH, SEQ, -1), ab[:P_ROWS].reshape(BATCH, SEQ, 32),
                                          *args, zero_dn, zero_ret, None, 4, 1)
            ys, _, _ = _pair_mixer(proj[P_ROWS:].reshape(DEC_BATCH, DEC_SEQ, -1),
                                   ab[P_ROWS:].reshape(DEC_BATCH, DEC_SEQ, 32),
                                   *args, state_deltanet[:, e], state_retention[:, e], rope_ret, 2, 2)
            new_dn.append(s_dn)
            new_ret.append(s_ret)
            y_in = jnp.concatenate([yp.reshape(P_ROWS, -1), ys.reshape(S_ROWS, -1)], axis=0)
            x = _matmul_residual(y_in, ev_w_out, e, x, mods4, l, 2, "even_out")
        else:
            o = l // 2
            pr = _matmul(h, od_w_in, lead=o, tn=832, name="odd_in")
            cq = _rms(pr[:, :Q_RANK], od_q_norm[o])
            ckv = _rms(pr[:, Q_RANK:Q_RANK + KV_RANK], od_kv_norm[o])
            krope = pr[:, Q_RANK + KV_RANK:]
            wq = od_w_q_up[o].reshape(Q_RANK, H_C, NOPE_DIM + ROPE_DIM)
            wq = jnp.concatenate([wq[:, :, :NOPE_DIM].reshape(Q_RANK, -1), wq[:, :, NOPE_DIM:].reshape(Q_RANK, -1)], axis=1)
            q = _matmul(cq.astype(BF16), wq, name="odd_q_up")
            qn = q[:, :H_C * NOPE_DIM].astype(BF16)
            qr = q[:, H_C * NOPE_DIM:]
            ckv_p = ckv[:P_ROWS].reshape(BATCH, SEQ, KV_RANK)
            kr_p = krope[:P_ROWS].reshape(BATCH, SEQ, ROPE_DIM)
            new_ckv.append(ckv_p)
            new_krope.append(kr_p)
            cos, sin = rope_mla
            qr_s = _apply_rope(qr[P_ROWS:].reshape(DEC_BATCH, DEC_SEQ, H_C, ROPE_DIM), cos[:, None, :], sin[:, None, :])
            kr_s = _apply_rope(krope[P_ROWS:].reshape(DEC_BATCH, DEC_SEQ, ROPE_DIM), cos, sin)
            ckv_s = jnp.concatenate([cache_ckv[:, o], ckv[P_ROWS:].reshape(DEC_BATCH, DEC_SEQ, KV_RANK)], axis=1)
            kr_s = jnp.concatenate([cache_krope[:, o], kr_s], axis=1)
            ckv_all = jnp.concatenate([ckv_p.reshape(P_ROWS, KV_RANK), ckv_s.reshape(-1, KV_RANK)], axis=0)
            kv = _matmul(ckv_all.astype(BF16), od_w_kv_up, lead=o, out_dtype=BF16, name="odd_kv_up")
            kv_p = kv[:P_ROWS].reshape(BATCH, SEQ, -1)
            kv_s = kv[P_ROWS:].reshape(DEC_BATCH, PAST_LEN + DEC_SEQ, -1)
            ap = _attention(qn[:P_ROWS].reshape(BATCH, SEQ, -1), qr[:P_ROWS].astype(BF16).reshape(BATCH, SEQ, -1),
                            kv_p, kr_p.astype(BF16))
            as_ = _attention(qn[P_ROWS:].reshape(DEC_BATCH, DEC_SEQ, -1), qr_s.astype(BF16).reshape(DEC_BATCH, DEC_SEQ, -1),
                             kv_s, kr_s.astype(BF16))
            y_in = jnp.concatenate([ap.reshape(P_ROWS, -1), as_.reshape(S_ROWS, -1)], axis=0)
            x = _matmul_residual(y_in, od_w_out, o, x, mods4, l, 2, "odd_out")
        h2 = _modnorm(x, gain_ffn, mods4, l, 3, 4)
        y = _moe(h2, router_w_pad, router_b, moe_w1, moe_w3, moe_w2, l)
        gt2 = mods4[l, :, 0, 5 * D_MODEL:]
        x = x + gt2[cond_row] * y

    yn = _final_norm(x, norm_final)
    return (yn[:P_ROWS].reshape(BATCH, SEQ, D_MODEL), yn[P_ROWS:].reshape(DEC_BATCH, DEC_SEQ, D_MODEL),
            jnp.stack(new_dn, axis=1), jnp.stack(new_ret, axis=1),
            jnp.stack(new_ckv, axis=1), jnp.stack(new_krope, axis=1))
```

```python
import functools
import math

import jax
import jax.numpy as jnp
from jax import lax
from jax.experimental import pallas as pl
from jax.experimental.pallas import tpu as pltpu

F32 = jnp.float32
BF16 = jnp.bfloat16

D_MODEL = 2048
BATCH = 32
SEQ = 256
DEPTH = 4
DEC_BATCH = 4
DEC_SEQ = 1024
PAST_LEN = 512
GRID_W = 64
EPS = 1e-6
H_A = 8
DK_A = 128
DV_A = 128
CONV_W = 5
H_B = 8
DK_B = 128
DV_B = 256
H_C = 16
Q_RANK = 512
KV_RANK = 256
NOPE_DIM = 128
ROPE_DIM = 64
V_DIM = 128
ROPE_THETA = 10000.0
N_EXPERTS = 16
N_GROUPS = 4
EXPERTS_PER_GROUP = N_EXPERTS // N_GROUPS
TOP_K = 2
D_EXPERT = 512

P_ROWS = BATCH * SEQ
S_ROWS = DEC_BATCH * DEC_SEQ
ROWS = P_ROWS + S_ROWS
N_COND = 8

LANES = 128
ROW_TILE = 512
VMEM_LIMIT = 48 * 1024 * 1024
CHUNK = 64
EPI_ROWS = 256
MOE_TILE = 256

EV_GATE_A = 3072
EV_Q_B = 4096
EV_K_B = 5120
EV_V_B = 6144
EV_GATE_B = 8192
EV_MAIN = 10240


def _cparams(sem):
    return pltpu.CompilerParams(dimension_semantics=sem, vmem_limit_bytes=VMEM_LIMIT)


def _cond_of_tile(i, tm):
    p_tiles = P_ROWS // tm
    per_req = DEC_SEQ // tm
    return jnp.where(i < p_tiles, 0, 1 + (i - p_tiles) // per_req)


def _mods_kernel(c_ref, w_ref, b_ref, o_ref):
    c = c_ref[...]
    s = (c * jax.nn.sigmoid(c)).astype(BF16)
    o_ref[0] = jnp.dot(s, w_ref[0].astype(BF16), preferred_element_type=F32) + b_ref[0]


def _mods(cond8, mod_w, mod_b):
    tn = 1024
    n = mod_w.shape[-1]
    return pl.pallas_call(
        _mods_kernel,
        grid=(DEPTH, n // tn),
        in_specs=[
            pl.BlockSpec((N_COND, D_MODEL), lambda l, j: (0, 0)),
            pl.BlockSpec((1, D_MODEL, tn), lambda l, j: (l, 0, j)),
            pl.BlockSpec((1, 1, tn), lambda l, j: (l, 0, j)),
        ],
        out_specs=pl.BlockSpec((1, N_COND, tn), lambda l, j: (l, 0, j)),
        out_shape=jax.ShapeDtypeStruct((DEPTH, N_COND, n), F32),
        compiler_params=_cparams(("arbitrary", "arbitrary")),
        name="adaln_mods",
    )(cond8, mod_w, mod_b.reshape(DEPTH, 1, n))


def _modnorm_kernel(x_ref, g_ref, sh_ref, sc_ref, o_ref):
    x = x_ref[...]
    y = x * lax.rsqrt(jnp.mean(x * x, axis=-1, keepdims=True) + EPS)
    y = y * g_ref[0]
    o_ref[...] = (y * (1.0 + sc_ref[0, 0]) + sh_ref[0, 0]).astype(o_ref.dtype)


def _modnorm2_kernel(x_ref, g_ref, sh_ref, sc_ref, o_ref, o32_ref):
    x = x_ref[...]
    y = x * lax.rsqrt(jnp.mean(x * x, axis=-1, keepdims=True) + EPS)
    y = y * g_ref[0]
    y = y * (1.0 + sc_ref[0, 0]) + sh_ref[0, 0]
    o_ref[...] = y.astype(o_ref.dtype)
    o32_ref[...] = y


def _modnorm(x, gain, mods4, l, seg_shift, seg_scale, with_f32=False):
    tm = ROW_TILE
    row_spec = pl.BlockSpec((tm, D_MODEL), lambda i: (i, 0))
    out_specs, out_shape = row_spec, jax.ShapeDtypeStruct((ROWS, D_MODEL), BF16)
    if with_f32:
        out_specs, out_shape = [row_spec, row_spec], [out_shape, jax.ShapeDtypeStruct((ROWS, D_MODEL), F32)]
    return pl.pallas_call(
        _modnorm2_kernel if with_f32 else _modnorm_kernel,
        grid=(ROWS // tm,),
        in_specs=[
            row_spec,
            pl.BlockSpec((1, 1, D_MODEL), lambda i: (l, 0, 0)),
            pl.BlockSpec((1, 1, 1, D_MODEL), lambda i: (l, _cond_of_tile(i, tm), 0, seg_shift)),
            pl.BlockSpec((1, 1, 1, D_MODEL), lambda i: (l, _cond_of_tile(i, tm), 0, seg_scale)),
        ],
        out_specs=out_specs,
        out_shape=out_shape,
        compiler_params=_cparams(("arbitrary",)),
        name="modnorm",
    )(x, gain, mods4, mods4)


def _rmsnorm_kernel(x_ref, g_ref, o_ref):
    x = x_ref[...]
    y = x * lax.rsqrt(jnp.mean(x * x, axis=-1, keepdims=True) + EPS)
    o_ref[...] = y * g_ref[...]


def _final_norm(x, gain):
    tm = ROW_TILE
    return pl.pallas_call(
        _rmsnorm_kernel,
        grid=(ROWS // tm,),
        in_specs=[pl.BlockSpec((tm, D_MODEL), lambda i: (i, 0)),
                  pl.BlockSpec((1, D_MODEL), lambda i: (0, 0))],
        out_specs=pl.BlockSpec((tm, D_MODEL), lambda i: (i, 0)),
        out_shape=jax.ShapeDtypeStruct((ROWS, D_MODEL), F32),
        compiler_params=_cparams(("arbitrary",)),
        name="final_norm",
    )(x, gain.reshape(1, D_MODEL))


def _load_b(b_ref):
    return b_ref[0] if len(b_ref.shape) == 3 else b_ref[...]


def _mm_kernel(a_ref, b_ref, o_ref, bscr):
    @pl.when(pl.program_id(1) == 0)
    def _():
        bscr[...] = _load_b(b_ref).astype(BF16)

    o_ref[...] = jnp.dot(a_ref[...], bscr[...], preferred_element_type=F32).astype(o_ref.dtype)


def _mm_res_kernel(a_ref, b_ref, x_ref, gt_ref, o_ref, bscr):
    @pl.when(pl.program_id(1) == 0)
    def _():
        bscr[...] = _load_b(b_ref).astype(BF16)

    y = jnp.dot(a_ref[...], bscr[...], preferred_element_type=F32)
    o_ref[...] = x_ref[...] + gt_ref[0, 0] * y


def _matmul(a, b, *, lead=None, out_dtype=F32, tm=ROW_TILE, tn=1024, name="matmul"):
    m, k = a.shape
    n = b.shape[-1]
    tn = min(tn, n)
    assert m % tm == 0 and n % tn == 0
    if lead is None:
        b_spec = pl.BlockSpec((k, tn), lambda j, i: (0, j))
    else:
        b_spec = pl.BlockSpec((1, k, tn), lambda j, i: (lead, 0, j))
    return pl.pallas_call(
        _mm_kernel,
        grid=(n // tn, m // tm),
        in_specs=[pl.BlockSpec((tm, k), lambda j, i: (i, 0)), b_spec],
        out_specs=pl.BlockSpec((tm, tn), lambda j, i: (i, j)),
        out_shape=jax.ShapeDtypeStruct((m, n), out_dtype),
        scratch_shapes=[pltpu.VMEM((k, tn), BF16)],
        compiler_params=_cparams(("arbitrary", "arbitrary")),
        name=name,
    )(a, b)


def _matmul_residual(a, b, lead, x, mods4, l, seg_gate, name):
    m, k = a.shape
    n = b.shape[-1]
    tm, tn = ROW_TILE, 1024
    return pl.pallas_call(
        _mm_res_kernel,
        grid=(n // tn, m // tm),
        in_specs=[
            pl.BlockSpec((tm, k), lambda j, i: (i, 0)),
            pl.BlockSpec((1, k, tn), lambda j, i: (lead, 0, j)),
            pl.BlockSpec((tm, tn), lambda j, i: (i, j)),
            pl.BlockSpec((1, 1, 1, tn),
                         lambda j, i: (l, _cond_of_tile(i, tm), 0, seg_gate * (D_MODEL // tn) + j)),
        ],
        out_specs=pl.BlockSpec((tm, tn), lambda j, i: (i, j)),
        out_shape=jax.ShapeDtypeStruct((m, n), F32),
        scratch_shapes=[pltpu.VMEM((k, tn), BF16)],
        compiler_params=_cparams(("arbitrary", "arbitrary")),
        name=name,
    )(a, b, x, mods4)


def _bf(x):
    return x.astype(BF16)


def _dot_nt(a, b):
    return lax.dot_general(_bf(a), _bf(b), (((1,), (1,)), ((), ())), preferred_element_type=F32)


def _dot_tn(a, b):
    return lax.dot_general(_bf(a), _bf(b), (((0,), (0,)), ((), ())), preferred_element_type=F32)


def _dot_sel(m01, x):
    hi = _bf(x)
    r1 = x - hi.astype(F32)
    mid = _bf(r1)
    lo = _bf(r1 - mid.astype(F32))
    mb = _bf(m01)
    acc = jnp.dot(mb, lo, preferred_element_type=F32)
    acc = acc + jnp.dot(mb, mid, preferred_element_type=F32)
    return acc + jnp.dot(mb, hi, preferred_element_type=F32)


def _split2(x):
    hi = _bf(x)
    return hi, _bf(x - hi.astype(F32))


def _dot_x3(a, b):
    ah, al = _split2(a)
    bh, bl = _split2(b)
    acc = jnp.dot(al, bh, preferred_element_type=F32)
    acc = acc + jnp.dot(ah, bl, preferred_element_type=F32)
    return acc + jnp.dot(ah, bh, preferred_element_type=F32)


def _chunk_masks(c, rev):
    row = lax.broadcasted_iota(jnp.int32, (c, c), 0)
    col = lax.broadcasted_iota(jnp.int32, (c, c), 1)
    incl = (row <= col) if rev else (row >= col)
    strict = (row < col) if rev else (row > col)
    incl_t = (row >= col) if rev else (row <= col)
    return row, col, incl, strict, incl_t


def _chunk_slice(ci):
    return pl.ds(pl.multiple_of(ci * CHUNK, CHUNK), CHUNK)


def _pick_lane(x, j):
    lane = lax.broadcasted_iota(jnp.int32, x.shape, 1)
    return jnp.sum(jnp.where(lane == j, x, 0.0), axis=1, keepdims=True)


def _delta_kernel(q_ref, k_ref, v_ref, gate_ref, gb_ref, s0_ref, nrm_ref, o_ref, sfin_ref,
                  u_scr, w_scr, a_scr, qd_scr, kd_scr, s_scr, of_scr, or_scr, *, n_chunks, hb, group):
    c = CHUNK
    n = n_chunks * c
    head0 = pl.program_id(1) * hb
    masks = [_chunk_masks(c, False), _chunk_masks(c, True)]
    ones_f = jnp.ones((c, c), F32)
    scale = DK_A ** -0.5

    def prep(cg, carry):
        chains = [(d, h, cg * group + j) for d in range(2) for h in range(hb) for j in range(group)]
        q, k, v, g, beta = [], [], [], [], []
        for d, h, ci in chains:
            sl = _chunk_slice(ci)
            hs = slice(h * DK_A, (h + 1) * DK_A)
            q.append(q_ref[0, sl, hs])
            k.append(k_ref[0, sl, hs])
            v.append(v_ref[0, sl, hs])
            gall = gb_ref[0, sl, :]
            g.append(_pick_lane(gall, d * H_A + head0 + h))
            beta.append(_pick_lane(gall, (2 + d) * H_A + head0 + h))
        nch = range(len(chains))
        g_bc = [jnp.broadcast_to(g[i], (c, c)) for i in nch]
        gc_col = [_dot_sel(masks[chains[i][0]][2].astype(F32), g_bc[i]) for i in nch]
        gc_row = [_dot_sel(ones_f, g_bc[i] * masks[chains[i][0]][4].astype(F32)) for i in nch]
        decay = []
        for i in nch:
            incl = masks[chains[i][0]][2]
            decay.append(jnp.where(incl, jnp.exp(jnp.where(incl, gc_col[i] - gc_row[i], 0.0)), 0.0))
        e_in = [jnp.exp(gc_col[i][:, :1]) for i in nch]
        g_tot = [jnp.sum(g[i], axis=0, keepdims=True) for i in nch]
        kb = [k[i] * beta[i] for i in nch]
        p = [-(_dot_nt(kb[i], k[i]) * jnp.where(masks[chains[i][0]][3], decay[i], 0.0)) for i in nch]
        eye = (masks[0][0] == masks[0][1]).astype(F32)
        t = [eye + p[i] for i in nch]
        for _ in range(int(math.log2(c)) - 1):
            p = [_dot_x3(p[i], p[i]) for i in nch]
            t = [t[i] + _dot_x3(t[i], p[i]) for i in nch]
        u = [_dot_x3(t[i], v[i] * beta[i]) for i in nch]
        w = [_dot_x3(t[i], kb[i] * e_in[i]) for i in nch]
        qs = [q[i] * scale for i in nch]
        a = [_dot_nt(qs[i], k[i]) * decay[i] for i in nch]
        for i in nch:
            d, h, ci = chains[i]
            sl = _chunk_slice(ci)
            idx = d * hb + h
            u_scr[idx, sl, :] = u[i]
            w_scr[idx, sl, :] = _bf(w[i])
            a_scr[idx, sl, :] = _bf(a[i])
            qd_scr[idx, sl, :] = _bf(qs[i] * e_in[i])
            kd_scr[idx, sl, :] = _bf(k[i] * jnp.exp(g_tot[i] - gc_col[i][:, :1]))
        return carry

    lax.fori_loop(0, n_chunks // group, prep, 0)

    for d in range(2):
        for h in range(hb):
            s_scr[d * hb + h] = s0_ref[0, d, h]

    def scan(i, carry):
        chains = [(d, h) for d in range(2) for h in range(hb)]
        sls = [_chunk_slice((n_chunks - 1 - i) if d == 1 else i) for d, h in chains]
        nch = range(len(chains))
        s = [s_scr[d * hb + h] for d, h in chains]
        sb = [_bf(x) for x in s]
        ws = [jnp.dot(w_scr[d * hb + h, sls[j], :], sb[j], preferred_element_type=F32)
              for j, (d, h) in enumerate(chains)]
        qsd = [jnp.dot(qd_scr[d * hb + h, sls[j], :], sb[j], preferred_element_type=F32)
               for j, (d, h) in enumerate(chains)]
        vb = [_bf(u_scr[d * hb + h, sls[j], :] - ws[j]) for j, (d, h) in enumerate(chains)]
        av = [jnp.dot(a_scr[d * hb + h, sls[j], :], vb[j], preferred_element_type=F32)
              for j, (d, h) in enumerate(chains)]
        kv = [lax.dot_general(kd_scr[d * hb + h, sls[j], :], vb[j], (((0,), (0,)), ((), ())),
                              preferred_element_type=F32) for j, (d, h) in enumerate(chains)]
        for j in nch:
            d, h = chains[j]
            gall = gb_ref[0, sls[j], :]
            g_tot = jnp.sum(_pick_lane(gall, d * H_A + head0 + h), axis=0, keepdims=True)
            o_dst = or_scr if d == 1 else of_scr
            o_dst[sls[j], h * DV_A:(h + 1) * DV_A] = qsd[j] + av[j]
            s_scr[d * hb + h] = s[j] * jnp.exp(g_tot) + kv[j]
        return carry

    lax.fori_loop(0, n_chunks, scan, 0)

    for d in range(2):
        for h in range(hb):
            sfin_ref[0, d, h] = s_scr[d * hb + h]

    def epilogue(bi, carry):
        sl = pl.ds(pl.multiple_of(bi * EPI_ROWS, EPI_ROWS), EPI_ROWS)
        for h in range(hb):
            hs = slice(h * DV_A, (h + 1) * DV_A)
            o = of_scr[sl, hs] + or_scr[sl, hs]
            y = o * lax.rsqrt(jnp.mean(o * o, axis=-1, keepdims=True) + EPS) * nrm_ref[...]
            gt = gate_ref[0, sl, hs]
            o_ref[0, sl, hs] = (y * (gt * jax.nn.sigmoid(gt))).astype(o_ref.dtype)
        return carry

    lax.fori_loop(0, n // EPI_ROWS, epilogue, 0)


def _delta_mixer(q, k, v, proj, gb, s0, dn_norm, hb, group):
    b, n, _ = q.shape
    n_chunks = n // CHUNK
    w = hb * DK_A
    hd = lambda bi, hi: (bi, 0, hi)
    st = lambda bi, hi: (bi, 0, hi, 0, 0)
    return pl.pallas_call(
        functools.partial(_delta_kernel, n_chunks=n_chunks, hb=hb, group=group),
        grid=(b, H_A // hb),
        in_specs=[
            pl.BlockSpec((1, n, w), hd),
            pl.BlockSpec((1, n, w), hd),
            pl.BlockSpec((1, n, w), hd),
            pl.BlockSpec((1, n, w), lambda bi, hi: (bi, 0, EV_GATE_A // w + hi)),
            pl.BlockSpec((1, n, 4 * H_A), lambda bi, hi: (bi, 0, 0)),
            pl.BlockSpec((1, 2, hb, DK_A, DV_A), st),
            pl.BlockSpec((1, DV_A), lambda bi, hi: (0, 0)),
        ],
        out_specs=[pl.BlockSpec((1, n, w), hd), pl.BlockSpec((1, 2, hb, DK_A, DV_A), st)],
        out_shape=[jax.ShapeDtypeStruct((b, n, H_A * DV_A), BF16),
                   jax.ShapeDtypeStruct((b, 2, H_A, DK_A, DV_A), F32)],
        scratch_shapes=[
            pltpu.VMEM((2 * hb, n, DV_A), F32),
            pltpu.VMEM((2 * hb, n, DK_A), BF16),
            pltpu.VMEM((2 * hb, n, CHUNK), BF16),
            pltpu.VMEM((2 * hb, n, DK_A), BF16),
            pltpu.VMEM((2 * hb, n, DK_A), BF16),
            pltpu.VMEM((2 * hb, DK_A, DV_A), F32),
            pltpu.VMEM((n, w), F32),
            pltpu.VMEM((n, w), F32),
        ],
        compiler_params=_cparams(("arbitrary", "arbitrary")),
        name="delta_mixer",
    )(q, k, v, proj, gb, s0, dn_norm.reshape(1, DV_A))


def _ret_kernel(lg_ref, q_ref, k_ref, v_ref, gate_ref, s0_ref, nrm_ref, o_ref, sfin_ref,
                s_scr, of_scr, or_scr, *, n_chunks, hb):
    c = CHUNK
    n = n_chunks * c
    head0 = pl.program_id(1) * hb
    kscale = DK_B ** -0.5
    pos = lax.broadcasted_iota(jnp.int32, (c, 1), 0)

    for d in range(2):
        for h in range(hb):
            s_scr[d * hb + h] = s0_ref[0, d, h]

    def scan(i, carry):
        chains = [(d, h) for d in range(2) for h in range(hb)]
        nch = range(len(chains))
        masks = [_chunk_masks(c, False), _chunk_masks(c, True)]
        dist = jnp.abs(masks[0][0] - masks[0][1]).astype(F32)
        steps = [(pos + 1).astype(F32), (c - pos).astype(F32)]
        sls = [_chunk_slice((n_chunks - 1 - i) if d == 1 else i) for d, h in chains]
        lg = [jnp.full((1, 1), lg_ref[d, head0 + h], F32) for d, h in chains]
        q = [q_ref[0, sls[j], h * DK_B:(h + 1) * DK_B] for j, (d, h) in enumerate(chains)]
        k = [k_ref[0, sls[j], h * DK_B:(h + 1) * DK_B] * kscale for j, (d, h) in enumerate(chains)]
        vb = [_bf(v_ref[0, sls[j], h * DV_B:(h + 1) * DV_B]) for j, (d, h) in enumerate(chains)]
        s = [s_scr[d * hb + h] for d, h in chains]
        qk = [_dot_nt(q[j], k[j]) for j in nch]
        qs = [jnp.dot(_bf(q[j] * jnp.exp(lg[j] * steps[chains[j][0]])), _bf(s[j]),
                      preferred_element_type=F32) for j in nch]
        kv = [lax.dot_general(_bf(k[j] * jnp.exp(lg[j] * (c - steps[chains[j][0]]))), vb[j],
                              (((0,), (0,)), ((), ())), preferred_element_type=F32) for j in nch]
        a = [_bf(qk[j] * jnp.where(masks[chains[j][0]][2], jnp.exp(lg[j] * dist), 0.0)) for j in nch]
        av = [jnp.dot(a[j], vb[j], preferred_element_type=F32) for j in nch]
        for j in nch:
            d, h = chains[j]
            o_dst = or_scr if d == 1 else of_scr
            o_dst[sls[j], h * DV_B:(h + 1) * DV_B] = qs[j] + av[j]
            s_scr[d * hb + h] = s[j] * jnp.exp(lg[j] * c) + kv[j]
        return carry

    lax.fori_loop(0, n_chunks, scan, 0)

    for d in range(2):
        for h in range(hb):
            sfin_ref[0, d, h] = s_scr[d * hb + h]

    def epilogue(bi, carry):
        sl = pl.ds(pl.multiple_of(bi * EPI_ROWS, EPI_ROWS), EPI_ROWS)
        for h in range(hb):
            hs = slice(h * DV_B, (h + 1) * DV_B)
            r = of_scr[sl, hs] + or_scr[sl, hs]
            mu = jnp.mean(r, axis=-1, keepdims=True)
            rc = r - mu
            var = jnp.mean(rc * rc, axis=-1, keepdims=True)
            y = rc * lax.rsqrt(var + EPS) * nrm_ref[:, hs]
            gt = gate_ref[0, sl, hs]
            o_ref[0, sl, hs] = (y * (gt * jax.nn.sigmoid(gt))).astype(o_ref.dtype)
        return carry

    lax.fori_loop(0, n // EPI_ROWS, epilogue, 0)


def _ret_mixer(q, q_col, k, k_col, proj, log_decay, s0, ret_norm, hb):
    b, n, _ = proj.shape
    n_chunks = n // CHUNK
    wk = hb * DK_B
    wv = hb * DV_B
    st = lambda bi, hi: (bi, 0, hi, 0, 0)
    return pl.pallas_call(
        functools.partial(_ret_kernel, n_chunks=n_chunks, hb=hb),
        grid=(b, H_B // hb),
        in_specs=[
            pl.BlockSpec(memory_space=pltpu.SMEM),
            pl.BlockSpec((1, n, wk), lambda bi, hi: (bi, 0, q_col // wk + hi)),
            pl.BlockSpec((1, n, wk), lambda bi, hi: (bi, 0, k_col // wk + hi)),
            pl.BlockSpec((1, n, wv), lambda bi, hi: (bi, 0, EV_V_B // wv + hi)),
            pl.BlockSpec((1, n, wv), lambda bi, hi: (bi, 0, EV_GATE_B // wv + hi)),
            pl.BlockSpec((1, 2, hb, DK_B, DV_B), st),
            pl.BlockSpec((1, wv), lambda bi, hi: (0, hi)),
        ],
        out_specs=[pl.BlockSpec((1, n, wv), lambda bi, hi: (bi, 0, hi)),
                   pl.BlockSpec((1, 2, hb, DK_B, DV_B), st)],
        out_shape=[jax.ShapeDtypeStruct((b, n, H_B * DV_B), BF16),
                   jax.ShapeDtypeStruct((b, 2, H_B, DK_B, DV_B), F32)],
        scratch_shapes=[
            pltpu.VMEM((2 * hb, DK_B, DV_B), F32),
            pltpu.VMEM((n, wv), F32),
            pltpu.VMEM((n, wv), F32),
        ],
        compiler_params=_cparams(("arbitrary", "arbitrary")),
        name="ret_mixer",
    )(log_decay, q, k, proj, proj, s0, ret_norm.reshape(1, H_B * DV_B))


def _attn_kernel(qn_ref, qr_ref, kv_ref, kr_ref, o_ref):
    scale = (NOPE_DIM + ROPE_DIM) ** -0.5
    kr = kr_ref[0]

    def scores(h):
        qn = qn_ref[0, :, h * NOPE_DIM:(h + 1) * NOPE_DIM]
        qr = qr_ref[0, :, h * ROPE_DIM:(h + 1) * ROPE_DIM]
        kn = kv_ref[0, :, h * 256:h * 256 + NOPE_DIM]
        return (lax.dot_general(qn, kn, (((1,), (1,)), ((), ())), preferred_element_type=F32)
                + lax.dot_general(qr, kr, (((1,), (1,)), ((), ())), preferred_element_type=F32)) * scale

    s_next = scores(0)
    for h in range(H_C):
        s = s_next
        if h + 1 < H_C:
            s_next = scores(h + 1)
        vh = kv_ref[0, :, h * 256 + NOPE_DIM:(h + 1) * 256]
        m = jnp.max(s, axis=-1, keepdims=True)
        e = jnp.exp(s - m)
        p = e / jnp.sum(e, axis=-1, keepdims=True)
        o_ref[0, :, h * V_DIM:(h + 1) * V_DIM] = jnp.dot(
            _bf(p), vh, preferred_element_type=F32).astype(o_ref.dtype)


def _attention(qn, qr, kv, kr):
    b, nq, _ = qn.shape
    nk = kv.shape[1]
    tq = 256
    return pl.pallas_call(
        _attn_kernel,
        grid=(b, nq // tq),
        in_specs=[
            pl.BlockSpec((1, tq, H_C * NOPE_DIM), lambda bi, qi: (bi, qi, 0)),
            pl.BlockSpec((1, tq, H_C * ROPE_DIM), lambda bi, qi: (bi, qi, 0)),
            pl.BlockSpec((1, nk, H_C * 256), lambda bi, qi: (bi, 0, 0)),
            pl.BlockSpec((1, nk, ROPE_DIM), lambda bi, qi: (bi, 0, 0)),
        ],
        out_specs=pl.BlockSpec((1, tq, H_C * V_DIM), lambda bi, qi: (bi, qi, 0)),
        out_shape=jax.ShapeDtypeStruct((b, nq, H_C * V_DIM), BF16),
        compiler_params=_cparams(("arbitrary", "arbitrary")),
        name="mla_attention",
    )(qn, qr, kv, kr)


def _moe_kernel(te_ref, nv_ref, x_ref, w1_ref, w3_ref, w2_ref, o_ref, w1s, w3s, w2s):
    i = pl.program_id(0)
    prev = te_ref[jnp.maximum(i - 1, 0)]

    @pl.when((i == 0) | (te_ref[i] != prev))
    def _():
        w1s[...] = w1_ref[0, 0].astype(BF16)
        w3s[...] = w3_ref[0, 0].astype(BF16)
        w2s[...] = w2_ref[0, 0].astype(BF16)

    @pl.when(i < nv_ref[0])
    def _():
        x = x_ref[...]
        h1 = jnp.dot(x, w1s[...], preferred_element_type=F32)
        h3 = jnp.dot(x, w3s[...], preferred_element_type=F32)
        he = (h1 * jax.nn.sigmoid(h1)) * h3
        o_ref[...] = jnp.dot(_bf(he), w2s[...], preferred_element_type=F32)

    @pl.when(i >= nv_ref[0])
    def _():
        o_ref[...] = jnp.zeros_like(o_ref)


def _moe_experts(xg, tile_expert, n_valid, w1, w3, w2, l):
    r = xg.shape[0]
    tm = MOE_TILE
    wmap = lambda i, te, nv: (l, te[i], 0, 0)
    grid_spec = pltpu.PrefetchScalarGridSpec(
        num_scalar_prefetch=2,
        grid=(r // tm,),
        in_specs=[
            pl.BlockSpec((tm, D_MODEL), lambda i, te, nv: (i, 0)),
            pl.BlockSpec((1, 1, D_MODEL, D_EXPERT), wmap),
            pl.BlockSpec((1, 1, D_MODEL, D_EXPERT), wmap),
            pl.BlockSpec((1, 1, D_EXPERT, D_MODEL), wmap),
        ],
        out_specs=pl.BlockSpec((tm, D_MODEL), lambda i, te, nv: (i, 0)),
        scratch_shapes=[
            pltpu.VMEM((D_MODEL, D_EXPERT), BF16),
            pltpu.VMEM((D_MODEL, D_EXPERT), BF16),
            pltpu.VMEM((D_EXPERT, D_MODEL), BF16),
        ],
    )
    return pl.pallas_call(
        _moe_kernel,
        grid_spec=grid_spec,
        out_shape=jax.ShapeDtypeStruct((r, D_MODEL), F32),
        compiler_params=_cparams(("arbitrary",)),
        name="moe_experts",
    )(tile_expert, n_valid, xg, w1, w3, w2)


def _gather_kernel(idx_ref, src_ref, o_ref, buf, sem):
    tm = o_ref.shape[0]

    def row_copy(r):
        return pltpu.make_async_copy(src_ref.at[pl.ds(idx_ref[0, 0, r], 1), :], buf.at[pl.ds(r, 1), :], sem)

    def issue(r, carry):
        row_copy(r).start()
        return carry

    def drain(r, carry):
        row_copy(r).wait()
        return carry

    lax.fori_loop(0, tm, issue, 0)
    lax.fori_loop(0, tm, drain, 0)
    o_ref[...] = buf[...].astype(o_ref.dtype)


def _gather_rows(src, idx, out_dtype):
    n_out = idx.shape[0]
    d = src.shape[1]
    tm = MOE_TILE
    return pl.pallas_call(
        _gather_kernel,
        grid=(n_out // tm,),
        in_specs=[
            pl.BlockSpec((1, 1, tm), lambda i: (i, 0, 0), memory_space=pltpu.SMEM),
            pl.BlockSpec(memory_space=pl.ANY),
        ],
        out_specs=pl.BlockSpec((tm, d), lambda i: (i, 0)),
        out_shape=jax.ShapeDtypeStruct((n_out, d), out_dtype),
        scratch_shapes=[pltpu.VMEM((tm, d), F32), pltpu.SemaphoreType.DMA(())],
        compiler_params=_cparams(("arbitrary",)),
        name="row_gather",
    )(idx.reshape(n_out // tm, 1, tm), src)


def _cumsum_kernel(x_ref, o_ref, carry):
    @pl.when(pl.program_id(0) == 0)
    def _():
        carry[...] = jnp.zeros_like(carry)

    tm = x_ref.shape[0]
    row = lax.broadcasted_iota(jnp.int32, (tm, tm), 0)
    col = lax.broadcasted_iota(jnp.int32, (tm, tm), 1)
    tri = jnp.where(row >= col, 1.0, 0.0).astype(BF16)
    c = jnp.dot(tri, x_ref[...], preferred_element_type=F32) + carry[...]
    o_ref[...] = c
    carry[...] = c[tm - 1:tm, :]


def _cumsum_rows(x):
    r, w = x.shape
    tm = ROW_TILE
    return pl.pallas_call(
        _cumsum_kernel,
        grid=(r // tm,),
        in_specs=[pl.BlockSpec((tm, w), lambda i: (i, 0))],
        out_specs=pl.BlockSpec((tm, w), lambda i: (i, 0)),
        out_shape=jax.ShapeDtypeStruct((r, w), F32),
        scratch_shapes=[pltpu.VMEM((1, w), F32)],
        compiler_params=_cparams(("arbitrary",)),
        name="expert_rank",
    )(x)


def _first_max(x):
    n = x.shape[-1]
    idx = jnp.arange(n, dtype=jnp.int32)
    m = jnp.max(x, axis=-1, keepdims=True)
    first = jnp.min(jnp.where(x == m, idx, n), axis=-1, keepdims=True)
    return first[..., 0], idx == first


def _route(logits, router_b):
    t = logits.shape[0]
    scores = jax.nn.sigmoid(logits)
    s4 = (scores + router_b.astype(F32)).reshape(t, N_GROUPS, EXPERTS_PER_GROUP)
    _, hot1 = _first_max(s4)
    m1 = jnp.max(s4, axis=-1)
    m2 = jnp.max(jnp.where(hot1, -jnp.inf, s4), axis=-1)
    gidx, ghot = _first_max(m1 + m2)
    sel_g = jnp.sum(jnp.where(ghot[:, :, None], s4, 0.0), axis=1)
    sc_g = jnp.sum(jnp.where(ghot[:, :, None], scores.reshape(t, N_GROUPS, EXPERTS_PER_GROUP), 0.0), axis=1)
    e1, h1 = _first_max(sel_g)
    e2, h2 = _first_max(jnp.where(h1, -jnp.inf, sel_g))
    w1 = jnp.sum(jnp.where(h1, sc_g, 0.0), axis=-1)
    w2 = jnp.sum(jnp.where(h2, sc_g, 0.0), axis=-1)
    eidx = gidx[:, None] * EXPERTS_PER_GROUP + jnp.stack([e1, e2], axis=1)
    wts = jnp.stack([w1, w2], axis=1)
    return eidx.astype(jnp.int32), wts / (w1 + w2)[:, None]


def _moe(h, h32, router_w_pad, router_b, w1, w3, w2, l):
    t = h.shape[0]
    tm = MOE_TILE
    logits = _matmul(h, router_w_pad, tn=LANES, name="router")[:, :N_EXPERTS]
    eidx, wts = _route(logits, router_b)
    flat_e = eidx.reshape(-1)
    onehot = flat_e[:, None] == jnp.arange(LANES, dtype=jnp.int32)[None, :]
    csum = _cumsum_rows(onehot.astype(BF16))[:, :N_EXPERTS]
    oh = onehot[:, :N_EXPERTS]
    counts = csum[-1].astype(jnp.int32)
    padded = ((counts + tm - 1) // tm) * tm
    pad_end = jnp.cumsum(padded)
    pad_start = (pad_end - padded).astype(F32)
    dest = jnp.sum(jnp.where(oh, csum - 1.0 + pad_start[None, :], 0.0), axis=1).astype(jnp.int32)
    n_rows = 2 * t + N_EXPERTS * tm
    pair_tok = jnp.arange(2 * t, dtype=jnp.int32) // 2
    src_tok = jnp.zeros((n_rows,), jnp.int32).at[dest].set(pair_tok, unique_indices=True)
    n_tiles = n_rows // tm
    tile_start = jnp.arange(n_tiles, dtype=jnp.int32) * tm
    te = jnp.sum(tile_start[:, None] >= pad_end[None, :], axis=1)
    n_valid = (pad_end[-1] // tm).astype(jnp.int32)
    last_e = jnp.max(jnp.where(counts > 0, jnp.arange(N_EXPERTS), 0))
    te = jnp.where(jnp.arange(n_tiles) < n_valid, te, last_e).astype(jnp.int32)
    xg = _gather_rows(h32, src_tok, BF16)
    out = _moe_experts(xg, te, n_valid.reshape(1), w1, w3, w2, l)
    pos = dest.reshape(t, 2)
    return wts[:, 0:1] * out[pos[:, 0]] + wts[:, 1:2] * out[pos[:, 1]]


def _l2norm(x):
    return x * lax.rsqrt(jnp.sum(x * x, axis=-1, keepdims=True) + EPS)


def _rope_tables(n_rows, rot_dim):
    row = jnp.repeat(jnp.arange(n_rows, dtype=F32), GRID_W)
    col = jnp.tile(jnp.arange(GRID_W, dtype=F32), n_rows)
    n_freq = rot_dim // 4
    inv = jnp.power(ROPE_THETA, -jnp.arange(n_freq, dtype=F32) / n_freq)
    ang = jnp.concatenate([row[:, None] * inv, col[:, None] * inv], axis=-1)
    return jnp.cos(ang), jnp.sin(ang)


def _apply_rope(x, cos, sin):
    x1, x2 = jnp.split(x, 2, axis=-1)
    return jnp.concatenate([x1 * cos - x2 * sin, x2 * cos + x1 * sin], axis=-1)


def _dwconv(x, w):
    k, c = w.shape
    return lax.conv_general_dilated(x, w[:, None, :], window_strides=(1,), padding=[(k // 2, k // 2)],
                                    dimension_numbers=('NWC', 'WIO', 'NWC'), feature_group_count=c)


def _pair_mixer(proj, ab, conv_w, a_log, dt_bias, dn_norm, ret_log_decay, ret_norm, s_dn, s_ret, rope_tab,
                hb, group):
    b, n, _ = proj.shape
    qkv = jax.nn.silu(_dwconv(proj[..., :EV_GATE_A], conv_w))
    q_a = _l2norm(qkv[..., :1024].reshape(b, n, H_A, DK_A)).reshape(b, n, 1024)
    k_a = _l2norm(qkv[..., 1024:2048].reshape(b, n, H_A, DK_A)).reshape(b, n, 1024)
    v_a = qkv[..., 2048:3072]
    abr = ab.reshape(b, n, 2, 2, H_A)
    g = -jnp.exp(a_log) * jax.nn.softplus(abr[:, :, 0] + dt_bias)
    beta = jax.nn.sigmoid(abr[:, :, 1])
    gb = jnp.concatenate([g.reshape(b, n, 2 * H_A), beta.reshape(b, n, 2 * H_A)], axis=-1)
    o_a, s_dn_new = _delta_mixer(q_a, k_a, v_a, proj, gb, s_dn, dn_norm, hb, group)

    if rope_tab is None:
        o_r, s_ret_new = _ret_mixer(proj, EV_Q_B, proj, EV_K_B, proj, ret_log_decay, s_ret, ret_norm, hb)
    else:
        cos, sin = rope_tab
        rot = lambda t: _apply_rope(t.reshape(b, n, H_B, DK_B), cos[:, None, :], sin[:, None, :]).reshape(b, n, 1024)
        q_b = rot(proj[..., EV_Q_B:EV_K_B])
        k_b = rot(proj[..., EV_K_B:EV_V_B])
        o_r, s_ret_new = _ret_mixer(q_b, 0, k_b, 0, proj, ret_log_decay, s_ret, ret_norm, hb)
    return jnp.concatenate([o_a, o_r], axis=-1), s_dn_new, s_ret_new


def _rms(x, g):
    return x * lax.rsqrt(jnp.mean(x * x, axis=-1, keepdims=True) + EPS) * g


def kernel(x_prompt, x_sample, c, state_deltanet, state_retention, cache_ckv, cache_krope, c_ctx, mod_w, mod_b, norm_mix, norm_ffn, norm_final, ev_w_in, ev_conv, ev_a_log, ev_dt_bias, ev_dn_norm, ev_ret_log_decay, ev_ret_norm, ev_w_out, od_w_in, od_q_norm, od_kv_norm, od_w_q_up, od_w_kv_up, od_w_out, router_w, router_b, moe_w1, moe_w3, moe_w2):
    x = jnp.concatenate([x_prompt.reshape(P_ROWS, D_MODEL), x_sample.reshape(S_ROWS, D_MODEL)], axis=0)
    cond8 = jnp.concatenate([c_ctx[None, :], c, jnp.zeros((N_COND - 1 - DEC_BATCH, D_MODEL), F32)], axis=0)
    mods4 = _mods(cond8, mod_w, mod_b).reshape(DEPTH, N_COND, 1, 6 * D_MODEL)
    gain_mix = norm_mix.reshape(DEPTH, 1, D_MODEL)
    gain_ffn = norm_ffn.reshape(DEPTH, 1, D_MODEL)
    router_w_pad = jnp.pad(router_w, ((0, 0), (0, LANES - N_EXPERTS)))
    n_rows_grid = DEC_SEQ // GRID_W
    rope_ret = _rope_tables(n_rows_grid, DK_B)
    rope_mla = _rope_tables(n_rows_grid, ROPE_DIM)
    zero_dn = jnp.zeros((BATCH, 2, H_A, DK_A, DV_A), F32)
    zero_ret = jnp.zeros((BATCH, 2, H_B, DK_B, DV_B), F32)
    cond_row = jnp.concatenate([jnp.zeros((P_ROWS,), jnp.int32),
                                1 + jnp.arange(S_ROWS, dtype=jnp.int32) // DEC_SEQ])
    new_dn, new_ret, new_ckv, new_krope = [], [], [], []

    for l in range(DEPTH):
        h = _modnorm(x, gain_mix, mods4, l, 0, 1)
        if l % 2 == 0:
            e = l // 2
            w = ev_w_in[e]
            w_main = jnp.concatenate([w[:, :3072], w[:, 3104:]], axis=1)
            w_ab = jnp.pad(w[:, 3072:3104], ((0, 0), (0, LANES - 32)))
            proj = _matmul(h, w_main, name="even_in")
            ab = _matmul(h, w_ab, tn=LANES, name="even_ab")[:, :32]
            args = (ev_conv[e], ev_a_log[e], ev_dt_bias[e], ev_dn_norm[e], ev_ret_log_decay[e], ev_ret_norm[e])
            yp, s_dn, s_ret = _pair_mixer(proj[:P_ROWS].reshape(BATCH, SEQ, -1), ab[:P_ROWS].reshape(BATCH, SEQ, 32),
                                          *args, zero_dn, zero_ret, None, 4, 1)
            ys, _, _ = _pair_mixer(proj[P_ROWS:].reshape(DEC_BATCH, DEC_SEQ, -1),
                                   ab[P_ROWS:].reshape(DEC_BATCH, DEC_SEQ, 32),
                                   *args, state_deltanet[:, e], state_retention[:, e], rope_ret, 2, 2)
            new_dn.append(s_dn)
            new_ret.append(s_ret)
            y_in = jnp.concatenate([yp.reshape(P_ROWS, -1), ys.reshape(S_ROWS, -1)], axis=0)
            x = _matmul_residual(y_in, ev_w_out, e, x, mods4, l, 2, "even_out")
        else:
            o = l // 2
            pr = _matmul(h, od_w_in, lead=o, tn=832, name="odd_in")
            cq = _rms(pr[:, :Q_RANK], od_q_norm[o])
            ckv = _rms(pr[:, Q_RANK:Q_RANK + KV_RANK], od_kv_norm[o])
            krope = pr[:, Q_RANK + KV_RANK:]
            wq = od_w_q_up[o].reshape(Q_RANK, H_C, NOPE_DIM + ROPE_DIM)
            wq = jnp.concatenate([wq[:, :, :NOPE_DIM].reshape(Q_RANK, -1), wq[:, :, NOPE_DIM:].reshape(Q_RANK, -1)], axis=1)
            q = _matmul(cq.astype(BF16), wq, name="odd_q_up")
            qn = q[:, :H_C * NOPE_DIM].astype(BF16)
            qr = q[:, H_C * NOPE_DIM:]
            ckv_p = ckv[:P_ROWS].reshape(BATCH, SEQ, KV_RANK)
            kr_p = krope[:P_ROWS].reshape(BATCH, SEQ, ROPE_DIM)
            new_ckv.append(ckv_p)
            new_krope.append(kr_p)
            cos, sin = rope_mla
            qr_s = _apply_rope(qr[P_ROWS:].reshape(DEC_BATCH, DEC_SEQ, H_C, ROPE_DIM), cos[:, None, :], sin[:, None, :])
            kr_s = _apply_rope(krope[P_ROWS:].reshape(DEC_BATCH, DEC_SEQ, ROPE_DIM), cos, sin)
            ckv_s = jnp.concatenate([cache_ckv[:, o], ckv[P_ROWS:].reshape(DEC_BATCH, DEC_SEQ, KV_RANK)], axis=1)
            kr_s = jnp.concatenate([cache_krope[:, o], kr_s], axis=1)
            ckv_all = jnp.concatenate([ckv_p.reshape(P_ROWS, KV_RANK), ckv_s.reshape(-1, KV_RANK)], axis=0)
            kv = _matmul(ckv_all.astype(BF16), od_w_kv_up, lead=o, out_dtype=BF16, name="odd_kv_up")
            kv_p = kv[:P_ROWS].reshape(BATCH, SEQ, -1)
            kv_s = kv[P_ROWS:].reshape(DEC_BATCH, PAST_LEN + DEC_SEQ, -1)
            ap = _attention(qn[:P_ROWS].reshape(BATCH, SEQ, -1), qr[:P_ROWS].astype(BF16).reshape(BATCH, SEQ, -1),
                            kv_p, kr_p.astype(BF16))
            as_ = _attention(qn[P_ROWS:].reshape(DEC_BATCH, DEC_SEQ, -1), qr_s.astype(BF16).reshape(DEC_BATCH, DEC_SEQ, -1),
                             kv_s, kr_s.astype(BF16))
            y_in = jnp.concatenate([ap.reshape(P_ROWS, -1), as_.reshape(S_ROWS, -1)], axis=0)
            x = _matmul_residual(y_in, od_w_out, o, x, mods4, l, 2, "odd_out")
        h2, h2_f32 = _modnorm(x, gain_ffn, mods4, l, 3, 4, with_f32=True)
        y = _moe(h2, h2_f32, router_w_pad, router_b, moe_w1, moe_w3, moe_w2, l)
        gt2 = mods4[l, :, 0, 5 * D_MODEL:]
        x = x + gt2[cond_row] * y

    yn = _final_norm(x, norm_final)
    return (yn[:P_ROWS].reshape(BATCH, SEQ, D_MODEL), yn[P_ROWS:].reshape(DEC_BATCH, DEC_SEQ, D_MODEL),
            jnp.stack(new_dn, axis=1), jnp.stack(new_ret, axis=1),
            jnp.stack(new_ckv, axis=1), jnp.stack(new_krope, axis=1))
```
